```python
import jax, jax.numpy as jnp
from jax import lax
import numpy as np

D_MODEL = 1024
BATCH = 4
SEQ = 4096
DEPTH = 4
DEC_BATCH = 128
DEC_SEQ = 4
PAST_LEN = 2048
PAGE_SIZE = 128

N_EVEN = (DEPTH + 1) // 2
N_ODD = DEPTH // 2
D_FF = 2816
RMS_EPS = 1e-6
A_WIDTH = D_MODEL // 2
A_CHUNK = 128
A_GROUPS = 4
A_GROUP_DIM = A_WIDTH // A_GROUPS
B_HEAD_DIM = 64
B_WIDTH = D_MODEL - A_WIDTH
B_HEADS = B_WIDTH // B_HEAD_DIM
MOBA_BLOCK = 256
MOBA_TOPK = 3
MOBA_Q_BLOCK = 32
C_WIDTH = D_MODEL
CONV_W = 3
EVEN_IN = 2 * A_WIDTH + 3 * B_WIDTH
MIX_WIDTH = A_WIDTH + B_WIDTH

kernel_name = "hybrid_gmlp_moba_shortconv_macaron_step"


def rms_norm(x, g):
    xf = x.astype(jnp.float32)
    y = xf * lax.rsqrt(jnp.mean(xf * xf, axis=-1, keepdims=True) + RMS_EPS)
    return (y * g.astype(jnp.float32)).astype(x.dtype)


def swiglu(x, w_gate, w_up, w_down):
    return (jax.nn.silu(x @ w_gate) * (x @ w_up)) @ w_down


def alibi_slopes(n):
    return 2.0 ** (-8.0 * jnp.arange(1, n + 1, dtype=jnp.float32) / n)


def chunk_spatial_gate(u, v, w_s, b_s):
    bsz, t, _ = u.shape
    nc = t // A_CHUNK
    mask = jnp.tril(jnp.ones((A_CHUNK, A_CHUNK), dtype=bool))
    w = jnp.where(mask, w_s, 0).astype(v.dtype)
    vc = v.reshape(bsz, nc, A_CHUNK, A_GROUPS, A_GROUP_DIM)
    mixed = jnp.einsum('gts,bnsgc->bntgc', w, vc) + b_s.T.astype(v.dtype)[None, None, :, :, None]
    return u * mixed.reshape(bsz, t, A_WIDTH)


def gmlp_chunk_mixer(za, v_gain, w_s, b_s):
    z = jax.nn.gelu(za)
    u, v = jnp.split(z, 2, axis=-1)
    vg = v.reshape(v.shape[0], v.shape[1], A_GROUPS, A_GROUP_DIM)
    v = rms_norm(vg, v_gain.reshape(A_GROUPS, A_GROUP_DIM)).reshape(u.shape)
    t = u.shape[1]
    pad = (-t) % A_CHUNK
    up = jnp.pad(u, ((0, 0), (0, pad), (0, 0)))
    vp = jnp.pad(v, ((0, 0), (0, pad), (0, 0)))
    return chunk_spatial_gate(up, vp, w_s, b_s)[:, :t], v


def to_blocks(parts):
    bsz = parts[0].shape[0]
    length = sum(p.shape[1] for p in parts)
    pad = (-length) % MOBA_BLOCK
    zeros = jnp.zeros((bsz, pad) + parts[0].shape[2:], parts[0].dtype)
    full = jnp.concatenate(list(parts) + [zeros], axis=1)
    return full.reshape(bsz, -1, MOBA_BLOCK, B_HEADS, B_HEAD_DIM)


def moba_query_block(q, pos, k_blocks, v_blocks, k_mean, slopes):
    bsz, nq = q.shape[:2]
    nb = k_blocks.shape[1]
    n_sel = min(MOBA_TOPK, nb)
    qf = q.astype(jnp.float32)
    own = pos // MOBA_BLOCK
    gate = jnp.einsum('bqhd,bnhd->bhqn', qf, k_mean)
    fully_past = jnp.arange(nb)[None, :] < own[:, None]
    gate = jnp.where(fully_past[None, None], gate, -jnp.inf)
    _, top_idx = lax.top_k(gate, n_sel)
    slot_ok = jnp.arange(n_sel)[None, :] < own[:, None]
    own_idx = jnp.broadcast_to(own[None, None, :, None], (bsz, B_HEADS, nq, 1)).astype(top_idx.dtype)
    sel = jnp.concatenate([top_idx, own_idx], axis=-1)
    ok = jnp.concatenate([slot_ok, jnp.ones((nq, 1), dtype=bool)], axis=-1)
    b_ix = jnp.arange(bsz)[:, None, None, None]
    h_ix = jnp.arange(B_HEADS)[None, :, None, None]
    kg = k_blocks[b_ix, sel, :, h_ix, :].astype(jnp.float32)
    vg = v_blocks[b_ix, sel, :, h_ix, :].astype(jnp.float32)
    key_pos = sel[..., None] * MOBA_BLOCK + jnp.arange(MOBA_BLOCK)
    dist = (pos[None, None, :, None, None] - key_pos).astype(jnp.float32)
    s = jnp.einsum('bqhd,bhqskd->bhqsk', qf, kg) * (B_HEAD_DIM ** -0.5) \
        - slopes[None, :, None, None, None] * dist
    valid = ok[None, None, :, :, None] & (dist >= 0)
    s = jnp.where(valid, s, -jnp.inf)
    p = jax.nn.softmax(s.reshape(bsz, B_HEADS, nq, -1), axis=-1).reshape(s.shape)
    o = jnp.einsum('bhqsk,bhqskd->bqhd', p, vg)
    return o.astype(q.dtype)


def moba_prompt(q, k, v, slopes):
    bsz, t = q.shape[:2]
    kb = to_blocks([k])
    vb = to_blocks([v])
    k_mean = jnp.mean(kb.astype(jnp.float32), axis=2)
    nqb = t // MOBA_Q_BLOCK
    qs = q.reshape(bsz, nqb, MOBA_Q_BLOCK, B_HEADS, B_HEAD_DIM).swapaxes(0, 1)
    ps = jnp.arange(t, dtype=jnp.int32).reshape(nqb, MOBA_Q_BLOCK)
    out = lax.map(lambda a: moba_query_block(a[0], a[1], kb, vb, k_mean, slopes), (qs, ps))
    return out.swapaxes(0, 1).reshape(bsz, t, B_HEADS, B_HEAD_DIM)


def moba_sample(q, k_past, v_past, k_new, v_new, slopes):
    kb = to_blocks([k_past, k_new])
    vb = to_blocks([v_past, v_new])
    k_mean = jnp.mean(kb.astype(jnp.float32), axis=2)
    tq = q.shape[1]
    pos = (k_past.shape[1] + jnp.arange(tq, dtype=jnp.int32))[:, None]
    qs = q.swapaxes(0, 1)[:, :, None]
    out = lax.map(lambda a: moba_query_block(a[0], a[1], kb, vb, k_mean, slopes), (qs, pos))
    return out[:, :, 0].swapaxes(0, 1)


def short_conv_mixer(h, w_in, conv_w, w_out, conv_state):
    z = h @ w_in
    xin, b_gate, c_gate = jnp.split(z, 3, axis=-1)
    cx = c_gate * xin
    ext = jnp.concatenate([conv_state.astype(cx.dtype), cx], axis=1)
    t = cx.shape[1]
    conv = conv_w[0] * ext[:, 0:t]
    for j in range(1, CONV_W):
        conv = conv + conv_w[j] * ext[:, j:j + t]
    y = (b_gate * conv) @ w_out
    return y, ext[:, -(CONV_W - 1):]


def trunk(x, paged, conv_init, p):
    slopes = alibi_slopes(B_HEADS)
    new_k, new_v, new_conv, new_av = [], [], [], []
    for i in range(DEPTH):
        h = rms_norm(x, p['ffn1_norm'][i])
        x = x + 0.5 * swiglu(h, p['ffn1_gate'][i], p['ffn1_up'][i], p['ffn1_down'][i])
        h = rms_norm(x, p['mix_norm'][i])
        j = i // 2
        if i % 2 == 0:
            z = h @ p['even_w_in'][j]
            bsz, t = z.shape[:2]
            ya, va = gmlp_chunk_mixer(z[..., :2 * A_WIDTH], p['a_v_gain'][j], p['a_w_s'][j], p['a_b_s'][j])
            q, k, v = [a.reshape(bsz, t, B_HEADS, B_HEAD_DIM)
                       for a in jnp.split(z[..., 2 * A_WIDTH:], 3, axis=-1)]
            if paged is None:
                yb = moba_prompt(q, k, v, slopes)
            else:
                cache_k, cache_v, page_table = paged
                n_past = page_table.shape[1] * PAGE_SIZE
                k_past = cache_k[j][page_table].reshape(bsz, n_past, B_HEADS, B_HEAD_DIM)
                v_past = cache_v[j][page_table].reshape(bsz, n_past, B_HEADS, B_HEAD_DIM)
                yb = moba_sample(q, k_past, v_past, k, v, slopes)
                new_av.append(va)
            y = jnp.concatenate([ya, yb.reshape(bsz, t, B_WIDTH)], axis=-1) @ p['even_w_out'][j]
            new_k.append(k)
            new_v.append(v)
        else:
            y, cs = short_conv_mixer(h, p['odd_w_in'][j], p['odd_conv_w'][j], p['odd_w_out'][j], conv_init[j])
            new_conv.append(cs)
        x = x + y
        h = rms_norm(x, p['ffn2_norm'][i])
        x = x + 0.5 * swiglu(h, p['ffn2_gate'][i], p['ffn2_up'][i], p['ffn2_down'][i])
    return rms_norm(x, p['final_norm']), new_k, new_v, new_conv, new_av


def setup_inputs(seed: int = 0) -> dict:
    key = jax.random.key(seed)
    ks = jax.random.split(key, 32)
    n_pages = PAST_LEN // PAGE_SIZE
    n_used = DEC_BATCH * n_pages
    n_phys = n_used + n_used // 4
    f32 = jnp.float32
    nrm = lambda k, shape, s=1.0: jax.random.normal(k, shape, f32) * s
    gain = lambda k, shape: 1.0 + 0.02 * jax.random.normal(k, shape, f32)
    perm = jax.random.permutation(ks[5], n_phys)[:n_used]
    return {
        'x_prompt': nrm(ks[0], (BATCH, SEQ, D_MODEL)),
        'x_sample': nrm(ks[1], (DEC_BATCH, DEC_SEQ, D_MODEL)),
        'cache_k': nrm(ks[2], (N_EVEN, n_phys, PAGE_SIZE, B_HEADS, B_HEAD_DIM)),
        'cache_v': nrm(ks[3], (N_EVEN, n_phys, PAGE_SIZE, B_HEADS, B_HEAD_DIM)),
        'state_conv': nrm(ks[4], (N_ODD, DEC_BATCH, CONV_W - 1, C_WIDTH)),
        'page_table': perm.reshape(DEC_BATCH, n_pages).astype(jnp.int32),
        'ffn1_norm': gain(ks[6], (DEPTH, D_MODEL)),
        'ffn1_gate': nrm(ks[7], (DEPTH, D_MODEL, D_FF), D_MODEL ** -0.5),
        'ffn1_up': nrm(ks[8], (DEPTH, D_MODEL, D_FF), D_MODEL ** -0.5),
        'ffn1_down': nrm(ks[9], (DEPTH, D_FF, D_MODEL), D_FF ** -0.5),
        'mix_norm': gain(ks[10], (DEPTH, D_MODEL)),
        'ffn2_norm': gain(ks[11], (DEPTH, D_MODEL)),
        'ffn2_gate': nrm(ks[12], (DEPTH, D_MODEL, D_FF), D_MODEL ** -0.5),
        'ffn2_up': nrm(ks[13], (DEPTH, D_MODEL, D_FF), D_MODEL ** -0.5),
        'ffn2_down': nrm(ks[14], (DEPTH, D_FF, D_MODEL), D_FF ** -0.5),
        'even_w_in': nrm(ks[15], (N_EVEN, D_MODEL, EVEN_IN), D_MODEL ** -0.5),
        'even_w_out': nrm(ks[16], (N_EVEN, MIX_WIDTH, D_MODEL), MIX_WIDTH ** -0.5),
        'a_v_gain': gain(ks[17], (N_EVEN, A_WIDTH)),
        'a_w_s': nrm(ks[18], (N_EVEN, A_GROUPS, A_CHUNK, A_CHUNK), A_CHUNK ** -0.5),
        'a_b_s': gain(ks[19], (N_EVEN, A_GROUPS, A_CHUNK)),
        'odd_w_in': nrm(ks[20], (N_ODD, D_MODEL, 3 * C_WIDTH), D_MODEL ** -0.5),
        'odd_conv_w': nrm(ks[21], (N_ODD, CONV_W, C_WIDTH), CONV_W ** -0.5),
        'odd_w_out': nrm(ks[22], (N_ODD, C_WIDTH, D_MODEL), C_WIDTH ** -0.5),
        'final_norm': gain(ks[23], (D_MODEL,)),
    }


def reference(x_prompt, x_sample, cache_k, cache_v, state_conv, page_table,
              ffn1_norm, ffn1_gate, ffn1_up, ffn1_down, mix_norm,
              ffn2_norm, ffn2_gate, ffn2_up, ffn2_down,
              even_w_in, even_w_out, a_v_gain, a_w_s, a_b_s,
              odd_w_in, odd_conv_w, odd_w_out, final_norm):
    params = {
        'ffn1_norm': ffn1_norm, 'ffn1_gate': ffn1_gate, 'ffn1_up': ffn1_up, 'ffn1_down': ffn1_down,
        'mix_norm': mix_norm,
        'ffn2_norm': ffn2_norm, 'ffn2_gate': ffn2_gate, 'ffn2_up': ffn2_up, 'ffn2_down': ffn2_down,
        'even_w_in': even_w_in, 'even_w_out': even_w_out,
        'a_v_gain': a_v_gain, 'a_w_s': a_w_s, 'a_b_s': a_b_s,
        'odd_w_in': odd_w_in, 'odd_conv_w': odd_conv_w, 'odd_w_out': odd_w_out,
        'final_norm': final_norm,
    }
    conv_zero = jnp.zeros((N_ODD, x_prompt.shape[0], CONV_W - 1, C_WIDTH), x_prompt.dtype)
    y_prompt, kp, vp, cp, _ = trunk(x_prompt, None, conv_zero, params)
    y_sample, ks_, vs_, cs_, avs = trunk(x_sample, (cache_k, cache_v, page_table), state_conv, params)
    return (y_prompt, y_sample,
            jnp.stack(kp), jnp.stack(vp), jnp.stack(cp),
            jnp.stack(ks_), jnp.stack(vs_), jnp.stack(cs_), jnp.stack(avs))
```

```python
import functools

import jax
import jax.numpy as jnp
from jax import lax
from jax.experimental import pallas as pl
from jax.experimental.pallas import tpu as pltpu

D_MODEL = 1024
BATCH = 4
SEQ = 4096
DEPTH = 4
DEC_BATCH = 128
DEC_SEQ = 4
PAST_LEN = 2048
PAGE_SIZE = 128
N_EVEN = (DEPTH + 1) // 2
N_ODD = DEPTH // 2
D_FF = 2816
RMS_EPS = 1e-6
A_WIDTH = D_MODEL // 2
A_CHUNK = 128
A_GROUPS = 4
A_GROUP_DIM = A_WIDTH // A_GROUPS
B_HEAD_DIM = 64
B_WIDTH = D_MODEL - A_WIDTH
B_HEADS = B_WIDTH // B_HEAD_DIM
MOBA_BLOCK = 256
MOBA_TOPK = 3
C_WIDTH = D_MODEL
CONV_W = 3

N_PROMPT = BATCH * SEQ
N_SAMPLE = DEC_BATCH * DEC_SEQ
N_TOK = N_PROMPT + N_SAMPLE
TM = 512
N_TILES = N_TOK // TM
N_PROMPT_TILES = N_PROMPT // TM
TILES_PER_SEQ = SEQ // TM
FF_CHUNK = 256
N_FF_CHUNKS = D_FF // FF_CHUNK
N_PAGES = PAST_LEN // PAGE_SIZE
N_BLOCKS = SEQ // MOBA_BLOCK
N_PAST_BLOCKS = PAST_LEN // MOBA_BLOCK
PAGES_PER_BLOCK = MOBA_BLOCK // PAGE_SIZE
NEW_PAD = 16
HEAD_PAIR = 2 * B_HEAD_DIM
N_HEAD_PAIRS = B_HEADS // 2
SLOPES = tuple(2.0 ** (-8.0 * (h + 1) / B_HEADS) for h in range(B_HEADS))
NEG = -1e30
VMEM_LIMIT = 56 * 1024 * 1024

assert N_SAMPLE == TM and N_TOK % TM == 0 and D_FF % FF_CHUNK == 0

_NT = (((1,), (1,)), ((), ()))
bf16 = jnp.bfloat16
f32 = jnp.float32


def _log2(n):
    assert n & (n - 1) == 0
    return n.bit_length() - 1


def _rms(x, g):
    ms = jnp.mean(x * x, axis=-1, keepdims=True)
    return x * lax.rsqrt(ms + RMS_EPS) * g


def _params(sem):
    return pltpu.CompilerParams(dimension_semantics=sem, vmem_limit_bytes=VMEM_LIMIT)


def _const_spec(shape):
    nd = len(shape)
    return pl.BlockSpec(shape, lambda *_: (0,) * nd, pipeline_mode=pl.Buffered(1))


def _tile_spec(width):
    return pl.BlockSpec((TM, width), lambda i: (i, 0))


def _ffn_kernel(x_ref, g_ref, wg_ref, wu_ref, wd_ref, fg_ref, o_ref, acc_ref, *, final):
    x = x_ref[...]
    h = _rms(x, g_ref[...]).astype(bf16)
    for c in range(N_FF_CHUNKS):
        sl = slice(c * FF_CHUNK, (c + 1) * FF_CHUNK)
        gate = jnp.dot(h, wg_ref[:, sl], preferred_element_type=f32)
        up = jnp.dot(h, wu_ref[:, sl], preferred_element_type=f32)
        act = (gate / (1.0 + jnp.exp(-gate)) * up).astype(bf16)
        down = jnp.dot(act, wd_ref[sl, :], preferred_element_type=f32)
        if c == 0:
            acc_ref[...] = down
        else:
            acc_ref[...] += down
    y = x + 0.5 * acc_ref[...]
    if final:
        y = _rms(y, fg_ref[...])
    o_ref[...] = y


def _ffn(x, g, wg, wu, wd, fg, final):
    return pl.pallas_call(
        functools.partial(_ffn_kernel, final=final),
        grid=(N_TILES,),
        in_specs=[_tile_spec(D_MODEL), _const_spec((1, D_MODEL)),
                  _const_spec((D_MODEL, D_FF)), _const_spec((D_MODEL, D_FF)),
                  _const_spec((D_FF, D_MODEL)), _const_spec((1, D_MODEL))],
        out_specs=_tile_spec(D_MODEL),
        out_shape=jax.ShapeDtypeStruct((N_TOK, D_MODEL), f32),
        scratch_shapes=[pltpu.VMEM((TM, D_MODEL), f32)],
        compiler_params=_params(("parallel",)),
        name="ffn",
    )(x, g, wg, wu, wd, fg)


EVEN_IN = 2 * A_WIDTH + 3 * B_WIDTH


def _even_in_kernel(x_ref, g_ref, w_ref, gain_ref, ws_ref, bs_ref,
                    ya_ref, vs_ref, q_ref, k_ref, v_ref, kb_ref, vb_ref, km_ref, z_ref):
    i = pl.program_id(0)
    h = _rms(x_ref[...], g_ref[...]).astype(bf16)
    z_ref[...] = jnp.dot(h, w_ref[...], preferred_element_type=f32)

    def gelu(a):
        return a * (0.5 * (1.0 + jnp.tanh(0.7978845608028654 * (a + 0.044715 * (a * a * a)))))

    shift = jnp.where(i < N_PROMPT_TILES, A_CHUNK.bit_length() - 1, DEC_SEQ.bit_length() - 1)
    row = lax.broadcasted_iota(jnp.int32, (TM, TM), 0)
    col = lax.broadcasted_iota(jnp.int32, (TM, TM), 1)
    keep = ((row >> shift) == (col >> shift)) & (col <= row)
    for g in range(A_GROUPS):
        gs = slice(g * A_GROUP_DIM, (g + 1) * A_GROUP_DIM)
        u = gelu(z_ref[:, gs])
        v = gelu(z_ref[:, A_WIDTH + g * A_GROUP_DIM:A_WIDTH + (g + 1) * A_GROUP_DIM])
        vn = _rms(v, gain_ref[:, gs])
        w = jnp.where(keep, ws_ref[0, g], 0.0).astype(bf16)
        mixed = jnp.dot(w, vn.astype(bf16), preferred_element_type=f32) + bs_ref[0, g]
        ya_ref[:, gs] = (u * mixed).astype(bf16)

        @pl.when(i == N_TILES - 1)
        def _():
            vs_ref[:, gs] = vn

    o = 2 * A_WIDTH
    q_ref[...] = z_ref[:, o:o + B_WIDTH]
    k = z_ref[:, o + B_WIDTH:o + 2 * B_WIDTH]
    v = z_ref[:, o + 2 * B_WIDTH:o + 3 * B_WIDTH]
    k_ref[...] = k
    v_ref[...] = v
    kb_ref[...] = k.astype(bf16)
    vb_ref[...] = v.astype(bf16)
    for blk in range(TM // MOBA_BLOCK):
        ks = jnp.sum(k[blk * MOBA_BLOCK:(blk + 1) * MOBA_BLOCK], axis=0, keepdims=True)
        km_ref[0, blk:blk + 1, :] = ks * (1.0 / MOBA_BLOCK)


def _even_in(x, g, w_in, gain, ws_tiled, bs_tiled):
    kind = lambda i: (i // N_PROMPT_TILES, 0, 0, 0)
    wide = lambda dt: jax.ShapeDtypeStruct((N_TOK, B_WIDTH), dt)
    return pl.pallas_call(
        _even_in_kernel,
        grid=(N_TILES,),
        in_specs=[_tile_spec(D_MODEL), _const_spec((1, D_MODEL)),
                  _const_spec((D_MODEL, EVEN_IN)), _const_spec((1, A_WIDTH)),
                  pl.BlockSpec((1, A_GROUPS, TM, TM), kind),
                  pl.BlockSpec((1, A_GROUPS, TM, 1), kind)],
        out_specs=[_tile_spec(A_WIDTH),
                   pl.BlockSpec((TM, A_WIDTH), lambda i: (0, 0)),
                   _tile_spec(B_WIDTH), _tile_spec(B_WIDTH), _tile_spec(B_WIDTH),
                   _tile_spec(B_WIDTH), _tile_spec(B_WIDTH),
                   pl.BlockSpec((1, TM // MOBA_BLOCK, B_WIDTH), lambda i: (i, 0, 0))],
        out_shape=[jax.ShapeDtypeStruct((N_TOK, A_WIDTH), bf16),
                   jax.ShapeDtypeStruct((N_SAMPLE, A_WIDTH), f32),
                   wide(f32), wide(f32), wide(f32), wide(bf16), wide(bf16),
                   jax.ShapeDtypeStruct((N_TILES, TM // MOBA_BLOCK, B_WIDTH), f32)],
        scratch_shapes=[pltpu.VMEM((TM, EVEN_IN), f32)],
        compiler_params=_params(("arbitrary",)),
        name="even_in",
    )(x, g, w_in, gain, ws_tiled, bs_tiled)


def _select_blocks(gate, n_valid, n_past):
    col = lax.broadcasted_iota(jnp.int32, gate.shape, 1)
    valid = col < n_valid
    g = jnp.where(valid, gate, -jnp.inf)
    sel = jnp.zeros(gate.shape, jnp.bool_)
    for _ in range(min(MOBA_TOPK, n_past)):
        mx = jnp.max(g, axis=1, keepdims=True)
        idx = jnp.min(jnp.where(g == mx, col, n_past), axis=1, keepdims=True)
        hit = col == idx
        sel = sel | (hit & valid)
        g = jnp.where(hit, -jnp.inf, g)
    return sel


def _moba_prompt_kernel(q_ref, k_ref, v_ref, km_ref, o_ref):
    i = pl.program_id(1)
    blk_shape = (MOBA_BLOCK, MOBA_BLOCK)
    row = lax.broadcasted_iota(jnp.int32, blk_shape, 0)
    col = lax.broadcasted_iota(jnp.int32, blk_shape, 1)
    key_minus_query = (col - row).astype(f32)
    causal = col <= row
    lane = lax.broadcasted_iota(jnp.int32, (MOBA_BLOCK, HEAD_PAIR), 1)
    first = lane < B_HEAD_DIM
    colb = lax.broadcasted_iota(jnp.int32, (MOBA_BLOCK, N_BLOCKS), 1)
    blocks_back = (i - colb).astype(f32) * float(MOBA_BLOCK)
    own = pl.multiple_of(i * MOBA_BLOCK, MOBA_BLOCK)

    for p in range(N_HEAD_PAIRS):
        ls = slice(p * HEAD_PAIR, (p + 1) * HEAD_PAIR)
        q2 = q_ref[:, ls]
        km2 = km_ref[0, :, ls]
        kd = k_ref[pl.ds(own, MOBA_BLOCK), ls]
        vd = v_ref[pl.ds(own, MOBA_BLOCK), ls]
        qm, bias, init = [], [], []
        for e in range(2):
            slope = SLOPES[2 * p + e]
            hm = first if e == 0 else jnp.logical_not(first)
            qmf = jnp.where(hm, q2, 0.0)
            qmb = (qmf * (B_HEAD_DIM ** -0.5)).astype(bf16)
            gate = lax.dot_general(qmf, km2, _NT, precision=lax.Precision.HIGHEST,
                                   preferred_element_type=f32)
            sel = _select_blocks(gate, i, N_BLOCKS - 1)
            bias.append(jnp.where(sel, -slope * blocks_back, NEG))
            qm.append(qmb)
            s = lax.dot_general(qmb, kd, _NT, preferred_element_type=f32) + slope * key_minus_query
            s = jnp.where(causal, s, NEG)
            m = jnp.max(s, axis=1, keepdims=True)
            pe = jnp.exp(s - m)
            l = jnp.sum(pe, axis=1, keepdims=True)
            acc = jnp.dot(pe.astype(bf16), vd, preferred_element_type=f32)
            init += [m, l, acc]

        def body(n, carry, ls=ls, qm=qm, bias=bias, p=p):
            start = pl.multiple_of(n * MOBA_BLOCK, MOBA_BLOCK)
            kn = k_ref[pl.ds(start, MOBA_BLOCK), ls]
            vn = v_ref[pl.ds(start, MOBA_BLOCK), ls]
            out = []
            for e in range(2):
                slope = SLOPES[2 * p + e]
                m, l, acc = carry[3 * e:3 * e + 3]
                bcol = jnp.sum(jnp.where(colb == n, bias[e], 0.0), axis=1, keepdims=True)
                s = (lax.dot_general(qm[e], kn, _NT, preferred_element_type=f32)
                     + slope * key_minus_query + bcol)
                m_new = jnp.maximum(m, jnp.max(s, axis=1, keepdims=True))
                alpha = jnp.exp(m - m_new)
                pe = jnp.exp(s - m_new)
                l = alpha * l + jnp.sum(pe, axis=1, keepdims=True)
                acc = alpha * acc + jnp.dot(pe.astype(bf16), vn, preferred_element_type=f32)
                out += [m_new, l, acc]
            return tuple(out)

        m0, l0, a0, m1, l1, a1 = lax.fori_loop(0, i, body, tuple(init))
        o_ref[:, ls] = jnp.where(first, a0 / l0, a1 / l1).astype(bf16)


def _moba_prompt(q, kb, vb, kmean):
    qspec = pl.BlockSpec((MOBA_BLOCK, B_WIDTH), lambda b, i: (b * N_BLOCKS + i, 0))
    kvspec = pl.BlockSpec((SEQ, B_WIDTH), lambda b, i: (b, 0))
    return pl.pallas_call(
        _moba_prompt_kernel,
        grid=(BATCH, N_BLOCKS),
        in_specs=[qspec, kvspec, kvspec,
                  pl.BlockSpec((1, N_BLOCKS, B_WIDTH), lambda b, i: (b, 0, 0))],
        out_specs=qspec,
        out_shape=jax.ShapeDtypeStruct((N_PROMPT, B_WIDTH), bf16),
        compiler_params=_params(("parallel", "arbitrary")),
        name="moba_prompt",
    )(q, kb, vb, kmean)


def _moba_sample_kernel(pt_ref, q_ref, kn_ref, vn_ref, *refs):
    k_refs = refs[:N_PAGES]
    v_refs = refs[N_PAGES:2 * N_PAGES]
    o_ref = refs[2 * N_PAGES]
    rows = DEC_SEQ * B_HEADS
    row = lax.broadcasted_iota(jnp.int32, (rows, B_WIDTH), 0)
    lane = lax.broadcasted_iota(jnp.int32, (rows, B_WIDTH), 1)
    head_lanes = (lane >> _log2(B_HEAD_DIM)) == (row & (B_HEADS - 1))
    qbd = jnp.where(head_lanes, q_ref[0], 0.0)
    qbd_b = (qbd * (B_HEAD_DIM ** -0.5)).astype(bf16)

    kms = []
    for blk in range(N_PAST_BLOCKS):
        tot = jnp.zeros((1, B_WIDTH), f32)
        for pg in range(blk * PAGES_PER_BLOCK, (blk + 1) * PAGES_PER_BLOCK):
            tot = tot + jnp.sum(k_refs[pg][...], axis=0, keepdims=True)
        kms.append(tot * (1.0 / MOBA_BLOCK))
    kmean = jnp.concatenate(kms, axis=0)
    gate = lax.dot_general(qbd, kmean, _NT, precision=lax.Precision.HIGHEST,
                           preferred_element_type=f32)
    sel = _select_blocks(gate, N_PAST_BLOCKS, N_PAST_BLOCKS)
    blk_bias = jnp.where(sel, 0.0, NEG)

    r1 = lax.broadcasted_iota(jnp.int32, (rows, 1), 0)
    hrow = r1 & (B_HEADS - 1)
    slope = jnp.zeros((rows, 1), f32)
    for h in range(B_HEADS):
        slope = jnp.where(hrow == h, SLOPES[h], slope)
    tq = r1 >> _log2(B_HEADS)
    kcol = lax.broadcasted_iota(jnp.int32, (rows, PAGE_SIZE), 1)

    scores = []
    for pg in range(N_PAGES):
        s = lax.dot_general(qbd_b, k_refs[pg][...].astype(bf16), _NT, preferred_element_type=f32)
        dist = (tq + (PAST_LEN - pg * PAGE_SIZE) - kcol).astype(f32)
        blk = pg // PAGES_PER_BLOCK
        scores.append(s - slope * dist + blk_bias[:, blk:blk + 1])
    tk = lax.broadcasted_iota(jnp.int32, (rows, NEW_PAD), 1)
    s_new = lax.dot_general(qbd_b, kn_ref[0].astype(bf16), _NT, preferred_element_type=f32)
    s_new = jnp.where(tk <= tq, s_new - slope * (tq - tk).astype(f32), NEG)

    m = jnp.max(s_new, axis=1, keepdims=True)
    for s in scores:
        m = jnp.maximum(m, jnp.max(s, axis=1, keepdims=True))
    p_new = jnp.exp(s_new - m)
    l = jnp.sum(p_new, axis=1, keepdims=True)
    acc = jnp.dot(p_new.astype(bf16), vn_ref[0].astype(bf16), preferred_element_type=f32)
    for pg in range(N_PAGES):
        pe = jnp.exp(scores[pg] - m)
        l = l + jnp.sum(pe, axis=1, keepdims=True)
        acc = acc + jnp.dot(pe.astype(bf16), v_refs[pg][...].astype(bf16),
                            preferred_element_type=f32)
    out = jnp.where(head_lanes, acc / l, 0.0)
    o_ref[0] = jnp.sum(out.reshape(DEC_SEQ, B_HEADS, B_WIDTH), axis=1)


def _moba_sample(page_table, q, k_new, v_new, cache_k, cache_v, layer):
    seq_spec = lambda r: pl.BlockSpec((1, r, B_WIDTH), lambda b, pt: (b, 0, 0))

    def page_spec(pg):
        return pl.BlockSpec((None, None, PAGE_SIZE, B_WIDTH),
                            lambda b, pt: (layer, pt[b * N_PAGES + pg], 0, 0))

    pages = [page_spec(pg) for pg in range(N_PAGES)]
    return pl.pallas_call(
        _moba_sample_kernel,
        grid_spec=pltpu.PrefetchScalarGridSpec(
            num_scalar_prefetch=1,
            grid=(DEC_BATCH,),
            in_specs=[seq_spec(DEC_SEQ * B_HEADS), seq_spec(NEW_PAD), seq_spec(NEW_PAD)]
            + pages + pages,
            out_specs=seq_spec(DEC_SEQ)),
        out_shape=jax.ShapeDtypeStruct((DEC_BATCH, DEC_SEQ, B_WIDTH), f32),
        compiler_params=_params(("parallel",)),
        name="moba_sample",
    )(page_table, q, k_new, v_new, *([cache_k] * N_PAGES), *([cache_v] * N_PAGES))


def _even_out_kernel(x_ref, ya_ref, yb_ref, w_ref, o_ref):
    y = jnp.dot(ya_ref[...], w_ref[:A_WIDTH, :], preferred_element_type=f32)
    y = y + jnp.dot(yb_ref[...], w_ref[A_WIDTH:, :], preferred_element_type=f32)
    o_ref[...] = x_ref[...] + y


def _even_out(x, ya, yb, w_out):
    return pl.pallas_call(
        _even_out_kernel,
        grid=(N_TILES,),
        in_specs=[_tile_spec(D_MODEL), _tile_spec(A_WIDTH), _tile_spec(B_WIDTH),
                  _const_spec((D_MODEL, D_MODEL))],
        out_specs=_tile_spec(D_MODEL),
        out_shape=jax.ShapeDtypeStruct((N_TOK, D_MODEL), f32),
        compiler_params=_params(("parallel",)),
        name="even_out",
    )(x, ya, yb, w_out)


CARRY = 8


def _odd_kernel(x_ref, g_ref, wi_ref, cw_ref, wo_ref, s1_ref, s2_ref,
                o_ref, cp_ref, cs_ref, z_ref, ext_ref):
    i = pl.program_id(0)
    h = _rms(x_ref[...], g_ref[...]).astype(bf16)
    z_ref[...] = jnp.dot(h, wi_ref[...], preferred_element_type=f32)

    @pl.when(i % TILES_PER_SEQ == 0)
    def _():
        ext_ref[:CARRY, :] = jnp.zeros((CARRY, C_WIDTH), f32)

    cx = z_ref[:, 2 * C_WIDTH:] * z_ref[:, :C_WIDTH]
    ext_ref[CARRY:, :] = cx
    back1 = ext_ref[CARRY - 1:CARRY - 1 + TM, :]
    back2 = ext_ref[CARRY - 2:CARRY - 2 + TM, :]
    t = lax.broadcasted_iota(jnp.int32, (TM, 1), 0) & (DEC_SEQ - 1)
    t = jnp.where(i < N_PROMPT_TILES, CONV_W, t)
    back1 = jnp.where(t >= 1, back1, s1_ref[...])
    back2 = jnp.where(t >= 2, back2, s2_ref[...])
    conv = cw_ref[0:1, :] * back2 + cw_ref[1:2, :] * back1 + cw_ref[2:3, :] * cx
    gated = (z_ref[:, C_WIDTH:2 * C_WIDTH] * conv).astype(bf16)
    o_ref[...] = x_ref[...] + jnp.dot(gated, wo_ref[...], preferred_element_type=f32)
    last = ext_ref[TM:, :]
    ext_ref[:CARRY, :] = last

    @pl.when(i < N_PROMPT_TILES)
    def _():
        cp_ref[0] = last

    @pl.when(i == N_TILES - 1)
    def _():
        cs_ref[...] = ext_ref[CARRY:, :]


def _odd(x, g, w_in, conv_w, w_out, s1, s2):
    return pl.pallas_call(
        _odd_kernel,
        grid=(N_TILES,),
        in_specs=[_tile_spec(D_MODEL), _const_spec((1, D_MODEL)),
                  _const_spec((D_MODEL, 3 * C_WIDTH)), _const_spec((CONV_W, C_WIDTH)),
                  _const_spec((C_WIDTH, D_MODEL)),
                  _const_spec((TM, C_WIDTH)), _const_spec((TM, C_WIDTH))],
        out_specs=[_tile_spec(D_MODEL),
                   pl.BlockSpec((1, CARRY, C_WIDTH),
                                lambda i: (jnp.minimum(i // TILES_PER_SEQ, BATCH - 1), 0, 0)),
                   pl.BlockSpec((TM, C_WIDTH), lambda i: (0, 0))],
        out_shape=[jax.ShapeDtypeStruct((N_TOK, D_MODEL), f32),
                   jax.ShapeDtypeStruct((BATCH, CARRY, C_WIDTH), f32),
                   jax.ShapeDtypeStruct((N_SAMPLE, C_WIDTH), f32)],
        scratch_shapes=[pltpu.VMEM((TM, 3 * C_WIDTH), f32),
                        pltpu.VMEM((CARRY + TM, C_WIDTH), f32)],
        compiler_params=_params(("arbitrary",)),
        name="odd_mixer",
    )(x, g, w_in, conv_w, w_out, s1, s2)


def kernel(x_prompt, x_sample, cache_k, cache_v, state_conv, page_table, ffn1_norm, ffn1_gate, ffn1_up, ffn1_down, mix_norm, ffn2_norm, ffn2_gate, ffn2_up, ffn2_down, even_w_in, even_w_out, a_v_gain, a_w_s, a_b_s, odd_w_in, odd_conv_w, odd_w_out, final_norm):
    x = jnp.concatenate([x_prompt.reshape(N_PROMPT, D_MODEL),
                         x_sample.reshape(N_SAMPLE, D_MODEL)], axis=0)
    n_phys = cache_k.shape[1]
    ck = cache_k.reshape(N_EVEN, n_phys, PAGE_SIZE, B_WIDTH)
    cv = cache_v.reshape(N_EVEN, n_phys, PAGE_SIZE, B_WIDTH)
    pt = page_table.reshape(-1).astype(jnp.int32)
    row = lambda a: a.reshape(1, -1)
    fg = row(final_norm)

    k_p, v_p, k_s, v_s, conv_p, conv_s, av_s = [], [], [], [], [], [], []
    for i in range(DEPTH):
        j = i // 2
        x = _ffn(x, row(ffn1_norm[i]), ffn1_gate[i].astype(bf16), ffn1_up[i].astype(bf16),
                 ffn1_down[i].astype(bf16), fg, False)
        if i % 2 == 0:
            reps = TM // A_CHUNK
            ws_tiled = jnp.stack([
                jnp.tile(a_w_s[j], (1, reps, reps)),
                jnp.tile(a_w_s[j][:, :DEC_SEQ, :DEC_SEQ], (1, DEC_BATCH, DEC_BATCH))])
            bs_tiled = jnp.stack([
                jnp.tile(a_b_s[j], (1, reps)),
                jnp.tile(a_b_s[j][:, :DEC_SEQ], (1, DEC_BATCH))])[..., None]
            ya, va, q, k, v, kb, vb, km = _even_in(
                x, row(mix_norm[i]), even_w_in[j].astype(bf16), row(a_v_gain[j]), ws_tiled, bs_tiled)
            kmean = km.reshape(N_TOK // MOBA_BLOCK, B_WIDTH)[:BATCH * N_BLOCKS]
            yb_p = _moba_prompt(q, kb, vb, kmean.reshape(BATCH, N_BLOCKS, B_WIDTH))
            new3 = lambda a: a[N_PROMPT:].reshape(DEC_BATCH, DEC_SEQ, B_WIDTH)
            padn = lambda a: jnp.pad(new3(a), ((0, 0), (0, NEW_PAD - DEC_SEQ), (0, 0)))
            q_rep = jnp.repeat(new3(q), B_HEADS, axis=1)
            yb_s = _moba_sample(pt, q_rep, padn(k), padn(v), ck, cv, j)
            yb = jnp.concatenate([yb_p, yb_s.reshape(N_SAMPLE, B_WIDTH).astype(bf16)], axis=0)
            x = _even_out(x, ya, yb, even_w_out[j].astype(bf16))
            heads = lambda a, lead: a.reshape(lead + (B_HEADS, B_HEAD_DIM))
            k_p.append(heads(k[:N_PROMPT], (BATCH, SEQ)))
            v_p.append(heads(v[:N_PROMPT], (BATCH, SEQ)))
            k_s.append(heads(k[N_PROMPT:], (DEC_BATCH, DEC_SEQ)))
            v_s.append(heads(v[N_PROMPT:], (DEC_BATCH, DEC_SEQ)))
            av_s.append(va.reshape(DEC_BATCH, DEC_SEQ, A_WIDTH))
        else:
            st = state_conv[j]
            zero = jnp.zeros((DEC_BATCH, 1, C_WIDTH), f32)
            s1 = jnp.concatenate([st[:, 1:2], zero, zero, zero], axis=1).reshape(N_SAMPLE, C_WIDTH)
            s2 = jnp.concatenate([st[:, 0:1], st[:, 1:2], zero, zero], axis=1).reshape(N_SAMPLE, C_WIDTH)
            x, cp, cs = _odd(x, row(mix_norm[i]), odd_w_in[j].astype(bf16), odd_conv_w[j],
                             odd_w_out[j].astype(bf16), s1, s2)
            conv_p.append(cp[:, CARRY - (CONV_W - 1):])
            conv_s.append(cs.reshape(DEC_BATCH, DEC_SEQ, C_WIDTH)[:, DEC_SEQ - (CONV_W - 1):])
        x = _ffn(x, row(ffn2_norm[i]), ffn2_gate[i].astype(bf16), ffn2_up[i].astype(bf16),
                 ffn2_down[i].astype(bf16), fg, i == DEPTH - 1)

    y_prompt = x[:N_PROMPT].reshape(BATCH, SEQ, D_MODEL)
    y_sample = x[N_PROMPT:].reshape(DEC_BATCH, DEC_SEQ, D_MODEL)
    return (y_prompt, y_sample, jnp.stack(k_p), jnp.stack(v_p), jnp.stack(conv_p),
            jnp.stack(k_s), jnp.stack(v_s), jnp.stack(conv_s), jnp.stack(av_s))
```

```python
import functools

import jax
import jax.numpy as jnp
from jax import lax
from jax.experimental import pallas as pl
from jax.experimental.pallas import tpu as pltpu

D_MODEL = 1024
BATCH = 4
SEQ = 4096
DEPTH = 4
DEC_BATCH = 128
DEC_SEQ = 4
PAST_LEN = 2048
PAGE_SIZE = 128
N_EVEN = (DEPTH + 1) // 2
N_ODD = DEPTH // 2
D_FF = 2816
RMS_EPS = 1e-6
A_WIDTH = D_MODEL // 2
A_CHUNK = 128
A_GROUPS = 4
A_GROUP_DIM = A_WIDTH // A_GROUPS
B_HEAD_DIM = 64
B_WIDTH = D_MODEL - A_WIDTH
B_HEADS = B_WIDTH // B_HEAD_DIM
MOBA_BLOCK = 256
MOBA_TOPK = 3
C_WIDTH = D_MODEL
CONV_W = 3

N_PROMPT = BATCH * SEQ
N_SAMPLE = DEC_BATCH * DEC_SEQ
N_TOK = N_PROMPT + N_SAMPLE
TM = 512
N_TILES = N_TOK // TM
N_PROMPT_TILES = N_PROMPT // TM
TILES_PER_SEQ = SEQ // TM
FF_CHUNK = 256
N_FF_CHUNKS = D_FF // FF_CHUNK
N_PAGES = PAST_LEN // PAGE_SIZE
N_BLOCKS = SEQ // MOBA_BLOCK
N_PAST_BLOCKS = PAST_LEN // MOBA_BLOCK
PAGES_PER_BLOCK = MOBA_BLOCK // PAGE_SIZE
BLOCKS_PER_TILE = TM // MOBA_BLOCK
NEW_PAD = 16
HEAD_PAIR = 2 * B_HEAD_DIM
N_HEAD_PAIRS = B_HEADS // 2
SLOPES = tuple(2.0 ** (-8.0 * (h + 1) / B_HEADS) for h in range(B_HEADS))
NEG = -1e30
SUBLANES = 8
LANES = 128
CONV_FRONT = 2 * DEC_BATCH
VMEM_LIMIT = 56 * 1024 * 1024

assert N_SAMPLE == TM and N_TOK % TM == 0 and D_FF % FF_CHUNK == 0
assert DEC_BATCH == LANES and CONV_W - 1 == 2 and DEC_SEQ >= CONV_W - 1

_NT = (((1,), (1,)), ((), ()))
_TN = (((0,), (0,)), ((), ()))
bf16 = jnp.bfloat16
f32 = jnp.float32


def _log2(n):
    assert n & (n - 1) == 0
    return n.bit_length() - 1


def _rms(x, g):
    ms = jnp.mean(x * x, axis=-1, keepdims=True)
    return x * lax.rsqrt(ms + RMS_EPS) * g


def _params(sem):
    return pltpu.CompilerParams(dimension_semantics=sem, vmem_limit_bytes=VMEM_LIMIT)


def _const_spec(shape):
    nd = len(shape)
    return pl.BlockSpec(shape, lambda *_: (0,) * nd, pipeline_mode=pl.Buffered(1))


def _tile_spec(width):
    return pl.BlockSpec((TM, width), lambda i: (i, 0))


def _prompt_tile_spec(width):
    return pl.BlockSpec((TM, width), lambda i: (jnp.minimum(i, N_PROMPT_TILES - 1), 0))


def _feature_major_spec():
    def index(i):
        j = jnp.minimum(i, N_PROMPT_TILES - 1)
        return (j // TILES_PER_SEQ, 0, j % TILES_PER_SEQ)
    return pl.BlockSpec((1, B_WIDTH, TM), index)


def _ffn_kernel(*refs, first, final):
    n_x = 2 if first else 1
    x_refs, (g_ref, wg_ref, wu_ref, wd_ref, fg_ref) = refs[:n_x], refs[n_x:n_x + 5]
    o_refs, acc_ref = refs[n_x + 5:-1], refs[-1]
    i = pl.program_id(0)
    is_prompt = i < N_PROMPT_TILES
    if first:
        x = jnp.where(is_prompt, x_refs[0][...], x_refs[1][...])
    else:
        x = x_refs[0][...]
    h = _rms(x, g_ref[...]).astype(bf16)
    for c in range(N_FF_CHUNKS):
        sl = slice(c * FF_CHUNK, (c + 1) * FF_CHUNK)
        gate = jnp.dot(h, wg_ref[:, sl], preferred_element_type=f32)
        up = jnp.dot(h, wu_ref[:, sl], preferred_element_type=f32)
        act = (gate / (1.0 + jnp.exp(-gate)) * up).astype(bf16)
        down = jnp.dot(act, wd_ref[sl, :], preferred_element_type=f32)
        if c == 0:
            acc_ref[...] = down
        else:
            acc_ref[...] += down
    y = x + 0.5 * acc_ref[...]
    if final:
        y = _rms(y, fg_ref[...])

        @pl.when(is_prompt)
        def _():
            o_refs[0][...] = y

        @pl.when(jnp.logical_not(is_prompt))
        def _():
            o_refs[1][...] = y
    else:
        o_refs[0][...] = y


def _ffn(xs, g, wg, wu, wd, fg, first=False, final=False):
    sample_spec = pl.BlockSpec((TM, D_MODEL), lambda i: (0, 0))
    x_specs = [_prompt_tile_spec(D_MODEL), sample_spec] if first else [_tile_spec(D_MODEL)]
    if final:
        out_specs = [_prompt_tile_spec(D_MODEL), sample_spec]
        out_shape = [jax.ShapeDtypeStruct((N_PROMPT, D_MODEL), f32),
                     jax.ShapeDtypeStruct((N_SAMPLE, D_MODEL), f32)]
    else:
        out_specs = [_tile_spec(D_MODEL)]
        out_shape = [jax.ShapeDtypeStruct((N_TOK, D_MODEL), f32)]
    return pl.pallas_call(
        functools.partial(_ffn_kernel, first=first, final=final),
        grid=(N_TILES,),
        in_specs=x_specs + [_const_spec((1, D_MODEL)),
                            _const_spec((D_MODEL, D_FF)), _const_spec((D_MODEL, D_FF)),
                            _const_spec((D_FF, D_MODEL)), _const_spec((1, D_MODEL))],
        out_specs=out_specs,
        out_shape=out_shape,
        scratch_shapes=[pltpu.VMEM((TM, D_MODEL), f32)],
        compiler_params=_params(("arbitrary",)),
        name="ffn",
    )(*xs, g, wg, wu, wd, fg)


ROW_COLS = 2 * A_WIDTH + B_WIDTH
T_ROWS = 3 * B_WIDTH


def _even_in_kernel(x_ref, g_ref, w_ref, wt_ref, gain_ref, ws_ref, bs_ref,
                    ya_ref, vs_ref, qt_ref, kt_ref, vt_ref, vtb_ref, kb_ref, km_ref, st_ref,
                    z_ref, zt_ref):
    i = pl.program_id(0)
    is_prompt = i < N_PROMPT_TILES
    h = _rms(x_ref[...], g_ref[...]).astype(bf16)
    z_ref[...] = jnp.dot(h, w_ref[...], preferred_element_type=f32)
    zt_ref[...] = lax.dot_general(wt_ref[...], h, _NT, preferred_element_type=f32)

    def gelu(a):
        return a * (0.5 * (1.0 + jnp.tanh(0.7978845608028654 * (a + 0.044715 * (a * a * a)))))

    row = lax.broadcasted_iota(jnp.int32, (TM, TM), 0)
    col = lax.broadcasted_iota(jnp.int32, (TM, TM), 1)
    chunk = lambda r: jnp.where(is_prompt, r >> _log2(A_CHUNK), r & (DEC_BATCH - 1))
    keep = (chunk(row) == chunk(col)) & (col <= row)
    for g in range(A_GROUPS):
        gs = slice(g * A_GROUP_DIM, (g + 1) * A_GROUP_DIM)
        u = gelu(z_ref[:, gs])
        v = gelu(z_ref[:, A_WIDTH + g * A_GROUP_DIM:A_WIDTH + (g + 1) * A_GROUP_DIM])
        vn = _rms(v, gain_ref[:, gs])
        w = jnp.where(keep, ws_ref[0, g], 0.0).astype(bf16)
        mixed = jnp.dot(w, vn.astype(bf16), preferred_element_type=f32) + bs_ref[0, g]
        ya_ref[:, gs] = (u * mixed).astype(bf16)

        @pl.when(i == N_TILES - 1)
        def _():
            vs_ref[:, gs] = vn

    k = z_ref[:, 2 * A_WIDTH:]
    kb_ref[...] = k.astype(bf16)
    for blk in range(BLOCKS_PER_TILE):
        ks = jnp.sum(k[blk * MOBA_BLOCK:(blk + 1) * MOBA_BLOCK], axis=0, keepdims=True)
        km_ref[0, blk:blk + 1, :] = ks * (1.0 / MOBA_BLOCK)

    @pl.when(is_prompt)
    def _():
        qt_ref[0] = zt_ref[:B_WIDTH, :]
        kt_ref[0] = zt_ref[B_WIDTH:2 * B_WIDTH, :]
        vt = zt_ref[2 * B_WIDTH:, :]
        vt_ref[0] = vt
        vtb_ref[0] = vt.astype(bf16)

    @pl.when(i == N_TILES - 1)
    def _():
        for part in range(3):
            for t in range(DEC_SEQ):
                st_ref[part, t] = zt_ref[part * B_WIDTH:(part + 1) * B_WIDTH,
                                         t * DEC_BATCH:(t + 1) * DEC_BATCH]


def _even_in(x, g, w_row, w_t, gain, ws_tiled, bs_tiled):
    kind = lambda i: (i // N_PROMPT_TILES, 0, 0, 0)
    fm = lambda dt: jax.ShapeDtypeStruct((BATCH, B_WIDTH, SEQ), dt)
    return pl.pallas_call(
        _even_in_kernel,
        grid=(N_TILES,),
        in_specs=[_tile_spec(D_MODEL), _const_spec((1, D_MODEL)),
                  _const_spec((D_MODEL, ROW_COLS)), _const_spec((T_ROWS, D_MODEL)),
                  _const_spec((1, A_WIDTH)),
                  pl.BlockSpec((1, A_GROUPS, TM, TM), kind),
                  pl.BlockSpec((1, A_GROUPS, TM, 1), kind)],
        out_specs=[_tile_spec(A_WIDTH),
                   pl.BlockSpec((TM, A_WIDTH), lambda i: (0, 0)),
                   _feature_major_spec(), _feature_major_spec(), _feature_major_spec(),
                   _feature_major_spec(),
                   _tile_spec(B_WIDTH),
                   pl.BlockSpec((1, BLOCKS_PER_TILE, B_WIDTH), lambda i: (i, 0, 0)),
                   pl.BlockSpec((3, DEC_SEQ, B_WIDTH, DEC_BATCH), lambda i: (0, 0, 0, 0))],
        out_shape=[jax.ShapeDtypeStruct((N_TOK, A_WIDTH), bf16),
                   jax.ShapeDtypeStruct((N_SAMPLE, A_WIDTH), f32),
                   fm(f32), fm(f32), fm(f32), fm(bf16),
                   jax.ShapeDtypeStruct((N_TOK, B_WIDTH), bf16),
                   jax.ShapeDtypeStruct((N_TILES, BLOCKS_PER_TILE, B_WIDTH), f32),
                   jax.ShapeDtypeStruct((3, DEC_SEQ, B_WIDTH, DEC_BATCH), f32)],
        scratch_shapes=[pltpu.VMEM((TM, ROW_COLS), f32), pltpu.VMEM((T_ROWS, TM), f32)],
        compiler_params=_params(("arbitrary",)),
        name="even_in",
    )(x, g, w_row, w_t, gain, ws_tiled, bs_tiled)


def _select_blocks(gate, n_valid, n_past, axis):
    blk = lax.broadcasted_iota(jnp.int32, gate.shape, axis)
    valid = blk < n_valid
    g = jnp.where(valid, gate, -jnp.inf)
    sel = jnp.zeros(gate.shape, jnp.bool_)
    for _ in range(min(MOBA_TOPK, n_past)):
        mx = jnp.max(g, axis=axis, keepdims=True)
        idx = jnp.min(jnp.where(g == mx, blk, n_past), axis=axis, keepdims=True)
        hit = blk == idx
        sel = sel | (hit & valid)
        g = jnp.where(hit, -jnp.inf, g)
    return sel


PAIR_COLS = 2 * MOBA_BLOCK
ALL_COLS = B_HEADS * MOBA_BLOCK


LOG2E = 1.4426950408889634


def _moba_prompt_kernel(q_ref, k_ref, v_ref, km_ref, o_ref,
                        rel_ref, qbd_ref, bias_ref, s0_ref, s1_ref, m_ref, l_ref, acc_ref):
    b = pl.program_id(0)
    i = pl.program_id(1)

    @pl.when((b == 0) & (i == 0))
    def _():
        key = lax.broadcasted_iota(jnp.int32, (MOBA_BLOCK, MOBA_BLOCK), 0)
        query = lax.broadcasted_iota(jnp.int32, (MOBA_BLOCK, MOBA_BLOCK), 1)
        kmq = (key - query).astype(f32)
        for p in range(N_HEAD_PAIRS):
            rel_ref[p] = jnp.concatenate([(SLOPES[2 * p] * LOG2E) * kmq,
                                          (SLOPES[2 * p + 1] * LOG2E) * kmq], axis=1)

    feat = lax.broadcasted_iota(jnp.int32, (HEAD_PAIR, PAIR_COLS), 0)
    qcol = lax.broadcasted_iota(jnp.int32, (HEAD_PAIR, PAIR_COLS), 1)
    mine = (feat >> _log2(B_HEAD_DIM)) == (qcol >> _log2(MOBA_BLOCK))
    blk = lax.broadcasted_iota(jnp.int32, (N_BLOCKS, PAIR_COLS), 0)
    bcol = lax.broadcasted_iota(jnp.int32, (N_BLOCKS, PAIR_COLS), 1)
    blocks_back = (i - blk).astype(f32) * float(MOBA_BLOCK)
    for p in range(N_HEAD_PAIRS):
        ls = slice(p * HEAD_PAIR, (p + 1) * HEAD_PAIR)
        q2 = q_ref[0, ls, :]
        qmf = jnp.where(mine, jnp.concatenate([q2, q2], axis=1), 0.0)
        qbd_ref[p] = (qmf * (LOG2E * B_HEAD_DIM ** -0.5)).astype(bf16)
        gate = jnp.dot(km_ref[0, :, ls], qmf, precision=lax.Precision.HIGHEST,
                       preferred_element_type=f32)
        sel = _select_blocks(gate, i, N_BLOCKS - 1, 0)
        slope = jnp.where(bcol < MOBA_BLOCK, SLOPES[2 * p] * LOG2E, SLOPES[2 * p + 1] * LOG2E)
        bias = jnp.where(sel, -slope * blocks_back, NEG)
        bias_ref[:, p * PAIR_COLS:(p + 1) * PAIR_COLS] = jnp.where(blk == i, 0.0, bias)
    m_ref[...] = jnp.full((1, ALL_COLS), NEG, f32)
    l_ref[...] = jnp.zeros((1, ALL_COLS), f32)
    acc_ref[...] = jnp.zeros((B_HEADS, B_HEAD_DIM, MOBA_BLOCK), f32)

    def scores(n, s_ref, own):
        start = pl.multiple_of(n * MOBA_BLOCK, MOBA_BLOCK)
        for p in range(N_HEAD_PAIRS):
            kn = k_ref[pl.ds(start, MOBA_BLOCK), p * HEAD_PAIR:(p + 1) * HEAD_PAIR]
            s = jnp.dot(kn, qbd_ref[p], preferred_element_type=f32) + rel_ref[p]
            if own:
                key = lax.broadcasted_iota(jnp.int32, (MOBA_BLOCK, PAIR_COLS), 0)
                query = lax.broadcasted_iota(jnp.int32, (MOBA_BLOCK, PAIR_COLS), 1) & (MOBA_BLOCK - 1)
                s = jnp.where(key <= query, s, NEG)
            s_ref[p] = s

    def absorb(n, s_ref):
        start = pl.multiple_of(n * MOBA_BLOCK, MOBA_BLOCK)
        for p in range(N_HEAD_PAIRS):
            cs = slice(p * PAIR_COLS, (p + 1) * PAIR_COLS)
            s = s_ref[p]
            brow = bias_ref[pl.ds(n, 1), cs]
            m_old = m_ref[:, cs]
            m_new = jnp.maximum(m_old, jnp.max(s, axis=0, keepdims=True) + brow)
            alpha = jnp.exp2(m_old - m_new)
            pe = jnp.exp2(s - (m_new - brow))
            l_ref[:, cs] = alpha * l_ref[:, cs] + jnp.sum(pe, axis=0, keepdims=True)
            m_ref[:, cs] = m_new
            peb = pe.astype(bf16)
            for e in range(2):
                h = 2 * p + e
                es = slice(e * MOBA_BLOCK, (e + 1) * MOBA_BLOCK)
                vn = v_ref[0, h * B_HEAD_DIM:(h + 1) * B_HEAD_DIM, pl.ds(start, MOBA_BLOCK)]
                acc_ref[h] = alpha[:, es] * acc_ref[h] + jnp.dot(
                    vn, peb[:, es], preferred_element_type=f32)

    scores(i, s0_ref, True)

    def two_steps(jj, carry):
        j = 2 * jj
        scores(j, s1_ref, False)
        absorb(jnp.where(jj == 0, i, j - 1), s0_ref)
        scores(j + 1, s0_ref, False)
        absorb(j, s1_ref)
        return carry

    lax.fori_loop(0, i // 2, two_steps, 0)

    @pl.when(i % 2 == 0)
    def _():
        absorb(jnp.maximum(i - 1, 0), s0_ref)

    @pl.when(i % 2 == 1)
    def _():
        scores(i - 1, s1_ref, False)
        absorb(jnp.where(i == 1, i, i - 2), s0_ref)
        absorb(i - 1, s1_ref)

    for h in range(B_HEADS):
        o_ref[0, h * B_HEAD_DIM:(h + 1) * B_HEAD_DIM, :] = (
            acc_ref[h] / l_ref[:, h * MOBA_BLOCK:(h + 1) * MOBA_BLOCK]).astype(bf16)


def _moba_prompt(q_t, kb, vtb, kmean):
    qspec = pl.BlockSpec((1, B_WIDTH, MOBA_BLOCK), lambda b, i: (b, 0, i))
    return pl.pallas_call(
        _moba_prompt_kernel,
        grid=(BATCH, N_BLOCKS),
        in_specs=[qspec,
                  pl.BlockSpec((SEQ, B_WIDTH), lambda b, i: (b, 0)),
                  pl.BlockSpec((1, B_WIDTH, SEQ), lambda b, i: (b, 0, 0)),
                  pl.BlockSpec((1, N_BLOCKS, B_WIDTH), lambda b, i: (b, 0, 0))],
        out_specs=qspec,
        out_shape=jax.ShapeDtypeStruct((BATCH, B_WIDTH, SEQ), bf16),
        scratch_shapes=[pltpu.VMEM((N_HEAD_PAIRS, MOBA_BLOCK, PAIR_COLS), f32),
                        pltpu.VMEM((N_HEAD_PAIRS, HEAD_PAIR, PAIR_COLS), bf16),
                        pltpu.VMEM((N_BLOCKS, ALL_COLS), f32),
                        pltpu.VMEM((N_HEAD_PAIRS, MOBA_BLOCK, PAIR_COLS), f32),
                        pltpu.VMEM((N_HEAD_PAIRS, MOBA_BLOCK, PAIR_COLS), f32),
                        pltpu.VMEM((1, ALL_COLS), f32),
                        pltpu.VMEM((1, ALL_COLS), f32),
                        pltpu.VMEM((B_HEADS, B_HEAD_DIM, MOBA_BLOCK), f32)],
        compiler_params=_params(("arbitrary", "arbitrary")),
        name="moba_prompt",
    )(q_t, kb, vtb, kmean)


def _moba_sample_kernel(pt_ref, q_ref, kn_ref, vn_ref, *refs):
    k_refs = refs[:N_PAGES]
    v_refs = refs[N_PAGES:2 * N_PAGES]
    o_ref = refs[2 * N_PAGES]
    rows = DEC_SEQ * B_HEADS
    row = lax.broadcasted_iota(jnp.int32, (rows, B_WIDTH), 0)
    lane = lax.broadcasted_iota(jnp.int32, (rows, B_WIDTH), 1)
    head_lanes = (lane >> _log2(B_HEAD_DIM)) == (row & (B_HEADS - 1))
    qbd = jnp.where(head_lanes, q_ref[0], 0.0)
    qbd_b = (qbd * (B_HEAD_DIM ** -0.5)).astype(bf16)

    flane = lax.broadcasted_iota(jnp.int32, (B_WIDTH, PAGE_SIZE), 1)
    kmean = jnp.zeros((B_WIDTH, PAGE_SIZE), f32)
    for blk in range(N_PAST_BLOCKS):
        tot = k_refs[blk * PAGES_PER_BLOCK][...]
        for pg in range(blk * PAGES_PER_BLOCK + 1, (blk + 1) * PAGES_PER_BLOCK):
            tot = tot + k_refs[pg][...]
        mean = jnp.sum(tot, axis=1, keepdims=True) * (1.0 / MOBA_BLOCK)
        kmean = jnp.where(flane == blk, mean, kmean)
    gate = jnp.dot(qbd, kmean, precision=lax.Precision.HIGHEST, preferred_element_type=f32)
    sel = _select_blocks(gate, N_PAST_BLOCKS, N_PAST_BLOCKS, 1)
    blk_bias = jnp.where(sel, 0.0, NEG)

    r1 = lax.broadcasted_iota(jnp.int32, (rows, 1), 0)
    hrow = r1 & (B_HEADS - 1)
    slope = jnp.zeros((rows, 1), f32)
    for h in range(B_HEADS):
        slope = jnp.where(hrow == h, SLOPES[h], slope)
    tq = r1 >> _log2(B_HEADS)
    kcol = lax.broadcasted_iota(jnp.int32, (rows, PAGE_SIZE), 1)

    scores = []
    for pg in range(N_PAGES):
        s = jnp.dot(qbd_b, k_refs[pg][...].astype(bf16), preferred_element_type=f32)
        dist = (tq + (PAST_LEN - pg * PAGE_SIZE) - kcol).astype(f32)
        blk = pg // PAGES_PER_BLOCK
        scores.append(s - slope * dist + blk_bias[:, blk:blk + 1])
    tk = lax.broadcasted_iota(jnp.int32, (rows, NEW_PAD), 1)
    s_new = lax.dot_general(qbd_b, kn_ref[0].astype(bf16), _NT, preferred_element_type=f32)
    s_new = jnp.where(tk <= tq, s_new - slope * (tq - tk).astype(f32), NEG)

    m = jnp.max(s_new, axis=1, keepdims=True)
    for s in scores:
        m = jnp.maximum(m, jnp.max(s, axis=1, keepdims=True))
    p_new = jnp.exp(s_new - m)
    l = jnp.sum(p_new, axis=1, keepdims=True)
    acc = jnp.dot(p_new.astype(bf16), vn_ref[0].astype(bf16), preferred_element_type=f32)
    for pg in range(N_PAGES):
        pe = jnp.exp(scores[pg] - m)
        l = l + jnp.sum(pe, axis=1, keepdims=True)
        acc = acc + lax.dot_general(pe.astype(bf16), v_refs[pg][...].astype(bf16), _NT,
                                    preferred_element_type=f32)
    out = jnp.where(head_lanes, acc / l, 0.0)
    o_ref[0] = jnp.sum(out.reshape(DEC_SEQ, B_HEADS, B_WIDTH), axis=1)


def _moba_sample(page_table, q, k_new, v_new, cache_k, cache_v, layer):
    seq_spec = lambda r: pl.BlockSpec((1, r, B_WIDTH), lambda b, pt: (b, 0, 0))

    def page_spec(pg):
        return pl.BlockSpec((None, None, B_WIDTH, PAGE_SIZE),
                            lambda b, pt: (layer, pt[b * N_PAGES + pg], 0, 0))

    pages = [page_spec(pg) for pg in range(N_PAGES)]
    return pl.pallas_call(
        _moba_sample_kernel,
        grid_spec=pltpu.PrefetchScalarGridSpec(
            num_scalar_prefetch=1,
            grid=(DEC_BATCH,),
            in_specs=[seq_spec(DEC_SEQ * B_HEADS), seq_spec(NEW_PAD), seq_spec(NEW_PAD)]
            + pages + pages,
            out_specs=seq_spec(DEC_SEQ)),
        out_shape=jax.ShapeDtypeStruct((DEC_BATCH, DEC_SEQ, B_WIDTH), f32),
        compiler_params=_params(("parallel",)),
        name="moba_sample",
    )(page_table, q, k_new, v_new, *([cache_k] * N_PAGES), *([cache_v] * N_PAGES))


def _even_out_kernel(x_ref, ya_ref, ybt_ref, ybs_ref, w_ref, o_ref):
    i = pl.program_id(0)
    is_prompt = i < N_PROMPT_TILES
    base = x_ref[...] + jnp.dot(ya_ref[...], w_ref[:A_WIDTH, :], preferred_element_type=f32)

    @pl.when(is_prompt)
    def _():
        o_ref[...] = base + lax.dot_general(ybt_ref[0], w_ref[A_WIDTH:, :], _TN,
                                            preferred_element_type=f32)

    @pl.when(jnp.logical_not(is_prompt))
    def _():
        o_ref[...] = base + jnp.dot(ybs_ref[...], w_ref[A_WIDTH:, :], preferred_element_type=f32)


def _even_out(x, ya, yb_t, yb_s, w_out):
    return pl.pallas_call(
        _even_out_kernel,
        grid=(N_TILES,),
        in_specs=[_tile_spec(D_MODEL), _tile_spec(A_WIDTH), _feature_major_spec(),
                  _const_spec((TM, B_WIDTH)), _const_spec((D_MODEL, D_MODEL))],
        out_specs=_tile_spec(D_MODEL),
        out_shape=jax.ShapeDtypeStruct((N_TOK, D_MODEL), f32),
        compiler_params=_params(("arbitrary",)),
        name="even_out",
    )(x, ya, yb_t, yb_s, w_out)


def _odd_kernel(x_ref, g_ref, wi_ref, cw_ref, wo_ref, st_ref,
                o_ref, cp_ref, cs_ref, z_ref, ext_ref):
    i = pl.program_id(0)
    is_prompt = i < N_PROMPT_TILES
    h = _rms(x_ref[...], g_ref[...]).astype(bf16)
    z_ref[...] = jnp.dot(h, wi_ref[...], preferred_element_type=f32)

    @pl.when(i % TILES_PER_SEQ == 0)
    def _():
        ext_ref[CONV_FRONT - SUBLANES:CONV_FRONT, :] = jnp.zeros((SUBLANES, C_WIDTH), f32)

    @pl.when(jnp.logical_not(is_prompt))
    def _():
        ext_ref[:CONV_FRONT, :] = st_ref[...]

    cx = z_ref[:, 2 * C_WIDTH:] * z_ref[:, :C_WIDTH]
    ext_ref[CONV_FRONT:, :] = cx
    back1 = jnp.where(is_prompt, ext_ref[CONV_FRONT - 1:CONV_FRONT - 1 + TM, :],
                      ext_ref[CONV_FRONT - DEC_BATCH:CONV_FRONT - DEC_BATCH + TM, :])
    back2 = jnp.where(is_prompt, ext_ref[CONV_FRONT - 2:CONV_FRONT - 2 + TM, :],
                      ext_ref[:TM, :])
    conv = cw_ref[0:1, :] * back2 + cw_ref[1:2, :] * back1 + cw_ref[2:3, :] * cx
    gated = (z_ref[:, C_WIDTH:2 * C_WIDTH] * conv).astype(bf16)
    o_ref[...] = x_ref[...] + jnp.dot(gated, wo_ref[...], preferred_element_type=f32)

    @pl.when(is_prompt)
    def _():
        last = ext_ref[CONV_FRONT + TM - SUBLANES:, :]
        ext_ref[CONV_FRONT - SUBLANES:CONV_FRONT, :] = last
        cp_ref[0] = last

    @pl.when(i == N_TILES - 1)
    def _():
        cs_ref[...] = ext_ref[TM:, :]


def _odd(x, g, w_in, conv_w, w_out, state):
    return pl.pallas_call(
        _odd_kernel,
        grid=(N_TILES,),
        in_specs=[_tile_spec(D_MODEL), _const_spec((1, D_MODEL)),
                  _const_spec((D_MODEL, 3 * C_WIDTH)), _const_spec((CONV_W, C_WIDTH)),
                  _const_spec((C_WIDTH, D_MODEL)), _const_spec((CONV_FRONT, C_WIDTH))],
        out_specs=[_tile_spec(D_MODEL),
                   pl.BlockSpec((1, SUBLANES, C_WIDTH),
                                lambda i: (jnp.minimum(i // TILES_PER_SEQ, BATCH - 1), 0, 0)),
                   pl.BlockSpec((CONV_FRONT, C_WIDTH), lambda i: (0, 0))],
        out_shape=[jax.ShapeDtypeStruct((N_TOK, D_MODEL), f32),
                   jax.ShapeDtypeStruct((BATCH, SUBLANES, C_WIDTH), f32),
                   jax.ShapeDtypeStruct((CONV_FRONT, C_WIDTH), f32)],
        scratch_shapes=[pltpu.VMEM((TM, 3 * C_WIDTH), f32),
                        pltpu.VMEM((CONV_FRONT + TM, C_WIDTH), f32)],
        compiler_params=_params(("arbitrary",)),
        name="odd_mixer",
    )(x, g, w_in, conv_w, w_out, state)


def _step_major(a):
    return jnp.swapaxes(a, 0, 1).reshape(a.shape[0] * a.shape[1], a.shape[2])


def _batch_major(a, steps=DEC_SEQ):
    return jnp.swapaxes(a.reshape(steps, DEC_BATCH, a.shape[-1]), 0, 1)


def kernel(x_prompt, x_sample, cache_k, cache_v, state_conv, page_table, ffn1_norm, ffn1_gate, ffn1_up, ffn1_down, mix_norm, ffn2_norm, ffn2_gate, ffn2_up, ffn2_down, even_w_in, even_w_out, a_v_gain, a_w_s, a_b_s, odd_w_in, odd_conv_w, odd_w_out, final_norm):
    n_phys = cache_k.shape[1]
    pages = lambda c: jnp.transpose(c, (0, 1, 3, 4, 2)).reshape(N_EVEN, n_phys, B_WIDTH, PAGE_SIZE)
    ck, cv = pages(cache_k), pages(cache_v)
    pt = page_table.reshape(-1).astype(jnp.int32)
    row = lambda a: a.reshape(1, -1)
    fg = row(final_norm)
    xs = (x_prompt.reshape(N_PROMPT, D_MODEL), _step_major(x_sample))

    k_p, v_p, k_s, v_s, conv_p, conv_s, av_s = [], [], [], [], [], [], []
    for i in range(DEPTH):
        j = i // 2
        (x,) = _ffn(xs, row(ffn1_norm[i]), ffn1_gate[i].astype(bf16), ffn1_up[i].astype(bf16),
                    ffn1_down[i].astype(bf16), fg, first=(i == 0))
        if i % 2 == 0:
            reps = TM // A_CHUNK
            w4 = a_w_s[j][:, :DEC_SEQ, :DEC_SEQ]
            ws_tiled = jnp.stack([
                jnp.tile(a_w_s[j], (1, reps, reps)),
                jnp.repeat(jnp.repeat(w4, DEC_BATCH, axis=1), DEC_BATCH, axis=2)])
            bs_tiled = jnp.stack([
                jnp.tile(a_b_s[j], (1, reps)),
                jnp.repeat(a_b_s[j][:, :DEC_SEQ], DEC_BATCH, axis=1)])[..., None]
            w_in = even_w_in[j].astype(bf16)
            w_row = jnp.concatenate([w_in[:, :2 * A_WIDTH],
                                     w_in[:, 2 * A_WIDTH + B_WIDTH:2 * A_WIDTH + 2 * B_WIDTH]], axis=1)
            w_t = w_in[:, 2 * A_WIDTH:].T
            ya, va, q_t, k_t, v_t, vtb, kb, km, st = _even_in(
                x, row(mix_norm[i]), w_row, w_t, row(a_v_gain[j]), ws_tiled, bs_tiled)
            kmean = km.reshape(N_TOK // MOBA_BLOCK, B_WIDTH)[:BATCH * N_BLOCKS]
            yb_t = _moba_prompt(q_t, kb, vtb, kmean.reshape(BATCH, N_BLOCKS, B_WIDTH))
            per_seq = jnp.transpose(st, (0, 3, 1, 2))
            padn = lambda a: jnp.pad(a, ((0, 0), (0, NEW_PAD - DEC_SEQ), (0, 0)))
            q_rep = jnp.repeat(per_seq[0], B_HEADS, axis=1)
            yb_s = _moba_sample(pt, q_rep, padn(per_seq[1]), padn(per_seq[2]), ck, cv, j)
            x = _even_out(x, ya, yb_t, _step_major(yb_s).astype(bf16), even_w_out[j].astype(bf16))
            k_p.append(k_t.reshape(BATCH, B_HEADS, B_HEAD_DIM, SEQ))
            v_p.append(v_t.reshape(BATCH, B_HEADS, B_HEAD_DIM, SEQ))
            k_s.append(st[1].reshape(DEC_SEQ, B_HEADS, B_HEAD_DIM, DEC_BATCH))
            v_s.append(st[2].reshape(DEC_SEQ, B_HEADS, B_HEAD_DIM, DEC_BATCH))
            av_s.append(_batch_major(va))
        else:
            state = jnp.swapaxes(state_conv[j], 0, 1).reshape(CONV_FRONT, C_WIDTH)
            x, cp, cs = _odd(x, row(mix_norm[i]), odd_w_in[j].astype(bf16), odd_conv_w[j],
                             odd_w_out[j].astype(bf16), state)
            conv_p.append(cp[:, SUBLANES - (CONV_W - 1):])
            conv_s.append(_batch_major(cs, CONV_W - 1))
        xs = _ffn((x,), row(ffn2_norm[i]), ffn2_gate[i].astype(bf16), ffn2_up[i].astype(bf16),
                  ffn2_down[i].astype(bf16), fg, final=(i == DEPTH - 1))

    y_prompt = xs[0].reshape(BATCH, SEQ, D_MODEL)
    y_sample = _batch_major(xs[1])
    to_rows_p = lambda ts: jnp.transpose(jnp.stack(ts), (0, 1, 4, 2, 3))
    to_rows_s = lambda ts: jnp.transpose(jnp.stack(ts), (0, 4, 1, 2, 3))
    return (y_prompt, y_sample, to_rows_p(k_p), to_rows_p(v_p), jnp.stack(conv_p),
            to_rows_s(k_s), to_rows_s(v_s), jnp.stack(conv_s), jnp.stack(av_s))
```

```python
import functools
import struct

import jax
import jax.numpy as jnp
from jax import lax
from jax.experimental import pallas as pl
from jax.experimental.pallas import tpu as pltpu

D_MODEL = 1024
BATCH = 4
SEQ = 4096
DEPTH = 4
DEC_BATCH = 128
DEC_SEQ = 4
PAST_LEN = 2048
PAGE_SIZE = 128
N_EVEN = (DEPTH + 1) // 2
N_ODD = DEPTH // 2
D_FF = 2816
RMS_EPS = 1e-6
A_WIDTH = D_MODEL // 2
A_CHUNK = 128
A_GROUPS = 4
A_GROUP_DIM = A_WIDTH // A_GROUPS
B_HEAD_DIM = 64
B_WIDTH = D_MODEL - A_WIDTH
B_HEADS = B_WIDTH // B_HEAD_DIM
MOBA_BLOCK = 256
MOBA_TOPK = 3
C_WIDTH = D_MODEL
CONV_W = 3

N_PROMPT = BATCH * SEQ
N_SAMPLE = DEC_BATCH * DEC_SEQ
N_TOK = N_PROMPT + N_SAMPLE
TM = 512
N_TILES = N_TOK // TM
N_PROMPT_TILES = N_PROMPT // TM
TILES_PER_SEQ = SEQ // TM
FF_CHUNK = 256
N_FF_CHUNKS = D_FF // FF_CHUNK
N_PAGES = PAST_LEN // PAGE_SIZE
N_BLOCKS = SEQ // MOBA_BLOCK
N_PAST_BLOCKS = PAST_LEN // MOBA_BLOCK
PAGES_PER_BLOCK = MOBA_BLOCK // PAGE_SIZE
BLOCKS_PER_TILE = TM // MOBA_BLOCK
NEW_PAD = 16
HEAD_PAIR = 2 * B_HEAD_DIM
N_HEAD_PAIRS = B_HEADS // 2
SLOPES = tuple(2.0 ** (-8.0 * (h + 1) / B_HEADS) for h in range(B_HEADS))
NEG = -1e30
SUBLANES = 8
LANES = 128
CONV_FRONT = 2 * DEC_BATCH
VMEM_LIMIT = 56 * 1024 * 1024

assert N_SAMPLE == TM and N_TOK % TM == 0 and D_FF % FF_CHUNK == 0
assert DEC_BATCH == LANES and CONV_W - 1 == 2 and DEC_SEQ >= CONV_W - 1

_NT = (((1,), (1,)), ((), ()))
_TN = (((0,), (0,)), ((), ()))
bf16 = jnp.bfloat16
f32 = jnp.float32


def _log2(n):
    assert n & (n - 1) == 0
    return n.bit_length() - 1


def _rms(x, g):
    ms = jnp.mean(x * x, axis=-1, keepdims=True)
    return x * lax.rsqrt(ms + RMS_EPS) * g


def _params(sem):
    return pltpu.CompilerParams(dimension_semantics=sem, vmem_limit_bytes=VMEM_LIMIT)


def _const_spec(shape):
    nd = len(shape)
    return pl.BlockSpec(shape, lambda *_: (0,) * nd, pipeline_mode=pl.Buffered(1))


def _tile_spec(width):
    return pl.BlockSpec((TM, width), lambda i: (i, 0))


def _prompt_tile_spec(width):
    return pl.BlockSpec((TM, width), lambda i: (jnp.minimum(i, N_PROMPT_TILES - 1), 0))


def _feature_major_spec():
    def index(i):
        j = jnp.minimum(i, N_PROMPT_TILES - 1)
        return (j // TILES_PER_SEQ, 0, j % TILES_PER_SEQ)
    return pl.BlockSpec((1, B_WIDTH, TM), index)


def _ffn_kernel(*refs, first, final):
    n_x = 2 if first else 1
    x_refs, (g_ref, wg_ref, wu_ref, wd_ref, fg_ref) = refs[:n_x], refs[n_x:n_x + 5]
    o_refs, acc_ref = refs[n_x + 5:-1], refs[-1]
    i = pl.program_id(0)
    is_prompt = i < N_PROMPT_TILES
    if first:
        x = jnp.where(is_prompt, x_refs[0][...], x_refs[1][...])
    else:
        x = x_refs[0][...]
    h = _rms(x, g_ref[...]).astype(bf16)
    for c in range(N_FF_CHUNKS):
        sl = slice(c * FF_CHUNK, (c + 1) * FF_CHUNK)
        gate = jnp.dot(h, wg_ref[:, sl].astype(bf16), preferred_element_type=f32)
        up = jnp.dot(h, wu_ref[:, sl].astype(bf16), preferred_element_type=f32)
        act = (gate / (1.0 + jnp.exp(-gate)) * up).astype(bf16)
        down = jnp.dot(act, wd_ref[sl, :].astype(bf16), preferred_element_type=f32)
        if c == 0:
            acc_ref[...] = down
        else:
            acc_ref[...] += down
    y = x + 0.5 * acc_ref[...]
    if final:
        y = _rms(y, fg_ref[...])

        @pl.when(is_prompt)
        def _():
            o_refs[0][...] = y

        @pl.when(jnp.logical_not(is_prompt))
        def _():
            o_refs[1][...] = y
    else:
        o_refs[0][...] = y


def _ffn(xs, g, wg, wu, wd, layer, fg, first=False, final=False):
    def layer_spec(rows, cols):
        return pl.BlockSpec((None, rows, cols), lambda i: (layer, 0, 0),
                            pipeline_mode=pl.Buffered(1))

    sample_spec = pl.BlockSpec((TM, D_MODEL), lambda i: (0, 0))
    x_specs = [_prompt_tile_spec(D_MODEL), sample_spec] if first else [_tile_spec(D_MODEL)]
    if final:
        out_specs = [_prompt_tile_spec(D_MODEL), sample_spec]
        out_shape = [jax.ShapeDtypeStruct((N_PROMPT, D_MODEL), f32),
                     jax.ShapeDtypeStruct((N_SAMPLE, D_MODEL), f32)]
    else:
        out_specs = [_tile_spec(D_MODEL)]
        out_shape = [jax.ShapeDtypeStruct((N_TOK, D_MODEL), f32)]
    return pl.pallas_call(
        functools.partial(_ffn_kernel, first=first, final=final),
        grid=(N_TILES,),
        in_specs=x_specs + [_const_spec((1, D_MODEL)),
                            layer_spec(D_MODEL, D_FF), layer_spec(D_MODEL, D_FF),
                            layer_spec(D_FF, D_MODEL), _const_spec((1, D_MODEL))],
        out_specs=out_specs,
        out_shape=out_shape,
        scratch_shapes=[pltpu.VMEM((TM, D_MODEL), f32)],
        compiler_params=_params(("arbitrary",)),
        name="ffn",
    )(*xs, g, wg, wu, wd, fg)


ROW_COLS = 2 * A_WIDTH + B_WIDTH
T_ROWS = 3 * B_WIDTH


def _even_in_kernel(x_ref, g_ref, w_ref, wt_ref, gain_ref, ws_ref, bs_ref,
                    ya_ref, vs_ref, qt_ref, kt_ref, vt_ref, vtb_ref, kb_ref, km_ref, st_ref,
                    z_ref, zt_ref):
    i = pl.program_id(0)
    is_prompt = i < N_PROMPT_TILES
    h = _rms(x_ref[...], g_ref[...]).astype(bf16)
    z_ref[...] = jnp.dot(h, w_ref[...], preferred_element_type=f32)
    zt_ref[...] = lax.dot_general(wt_ref[...], h, _NT, preferred_element_type=f32)

    def gelu(a):
        return a * (0.5 * (1.0 + jnp.tanh(0.7978845608028654 * (a + 0.044715 * (a * a * a)))))

    row = lax.broadcasted_iota(jnp.int32, (TM, TM), 0)
    col = lax.broadcasted_iota(jnp.int32, (TM, TM), 1)
    chunk = lambda r: jnp.where(is_prompt, r >> _log2(A_CHUNK), r & (DEC_BATCH - 1))
    keep = (chunk(row) == chunk(col)) & (col <= row)
    for g in range(A_GROUPS):
        gs = slice(g * A_GROUP_DIM, (g + 1) * A_GROUP_DIM)
        u = gelu(z_ref[:, gs])
        v = gelu(z_ref[:, A_WIDTH + g * A_GROUP_DIM:A_WIDTH + (g + 1) * A_GROUP_DIM])
        vn = _rms(v, gain_ref[:, gs])
        w = jnp.where(keep, ws_ref[0, g], 0.0).astype(bf16)
        mixed = jnp.dot(w, vn.astype(bf16), preferred_element_type=f32) + bs_ref[0, g]
        ya_ref[:, gs] = (u * mixed).astype(bf16)

        @pl.when(i == N_TILES - 1)
        def _():
            vs_ref[:, gs] = vn

    k = z_ref[:, 2 * A_WIDTH:]
    kb_ref[...] = k.astype(bf16)
    for blk in range(BLOCKS_PER_TILE):
        ks = jnp.sum(k[blk * MOBA_BLOCK:(blk + 1) * MOBA_BLOCK], axis=0, keepdims=True)
        km_ref[0, blk:blk + 1, :] = ks * (1.0 / MOBA_BLOCK)

    @pl.when(is_prompt)
    def _():
        qt_ref[0] = zt_ref[:B_WIDTH, :]
        kt_ref[0] = zt_ref[B_WIDTH:2 * B_WIDTH, :]
        vt = zt_ref[2 * B_WIDTH:, :]
        vt_ref[0] = vt
        vtb_ref[0] = vt.astype(bf16)

    @pl.when(i == N_TILES - 1)
    def _():
        for part in range(3):
            for t in range(DEC_SEQ):
                st_ref[part, t] = zt_ref[part * B_WIDTH:(part + 1) * B_WIDTH,
                                         t * DEC_BATCH:(t + 1) * DEC_BATCH]


def _even_in(x, g, w_row, w_t, gain, ws_tiled, bs_tiled):
    kind = lambda i: (i // N_PROMPT_TILES, 0, 0, 0)
    fm = lambda dt: jax.ShapeDtypeStruct((BATCH, B_WIDTH, SEQ), dt)
    return pl.pallas_call(
        _even_in_kernel,
        grid=(N_TILES,),
        in_specs=[_tile_spec(D_MODEL), _const_spec((1, D_MODEL)),
                  _const_spec((D_MODEL, ROW_COLS)), _const_spec((T_ROWS, D_MODEL)),
                  _const_spec((1, A_WIDTH)),
                  pl.BlockSpec((1, A_GROUPS, TM, TM), kind),
                  pl.BlockSpec((1, A_GROUPS, TM, 1), kind)],
        out_specs=[_tile_spec(A_WIDTH),
                   pl.BlockSpec((TM, A_WIDTH), lambda i: (0, 0)),
                   _feature_major_spec(), _feature_major_spec(), _feature_major_spec(),
                   _feature_major_spec(),
                   _tile_spec(B_WIDTH),
                   pl.BlockSpec((1, BLOCKS_PER_TILE, B_WIDTH), lambda i: (i, 0, 0)),
                   pl.BlockSpec((3, DEC_SEQ, B_WIDTH, DEC_BATCH), lambda i: (0, 0, 0, 0))],
        out_shape=[jax.ShapeDtypeStruct((N_TOK, A_WIDTH), bf16),
                   jax.ShapeDtypeStruct((N_SAMPLE, A_WIDTH), f32),
                   fm(f32), fm(f32), fm(f32), fm(bf16),
                   jax.ShapeDtypeStruct((N_TOK, B_WIDTH), bf16),
                   jax.ShapeDtypeStruct((N_TILES, BLOCKS_PER_TILE, B_WIDTH), f32),
                   jax.ShapeDtypeStruct((3, DEC_SEQ, B_WIDTH, DEC_BATCH), f32)],
        scratch_shapes=[pltpu.VMEM((TM, ROW_COLS), f32), pltpu.VMEM((T_ROWS, TM), f32)],
        compiler_params=_params(("arbitrary",)),
        name="even_in",
    )(x, g, w_row, w_t, gain, ws_tiled, bs_tiled)


def _select_blocks(gate, n_valid, n_past, axis):
    blk = lax.broadcasted_iota(jnp.int32, gate.shape, axis)
    valid = blk < n_valid
    g = jnp.where(valid, gate, -jnp.inf)
    sel = jnp.zeros(gate.shape, jnp.bool_)
    for _ in range(min(MOBA_TOPK, n_past)):
        mx = jnp.max(g, axis=axis, keepdims=True)
        idx = jnp.min(jnp.where(g == mx, blk, n_past), axis=axis, keepdims=True)
        hit = blk == idx
        sel = sel | (hit & valid)
        g = jnp.where(hit, -jnp.inf, g)
    return sel


PAIR_COLS = 2 * MOBA_BLOCK
ALL_COLS = B_HEADS * MOBA_BLOCK


LOG2E = 1.4426950408889634


def _round_bf16(x):
    bits = struct.unpack("<I", struct.pack("<f", x))[0]
    bits = (bits + 0x7FFF + ((bits >> 16) & 1)) & 0xFFFF0000
    return struct.unpack("<f", struct.pack("<I", bits))[0]


def _bf16_terms(x, n):
    terms = []
    for _ in range(n):
        t = _round_bf16(x)
        terms.append(t)
        x -= t
    return terms


SLOPE_TERMS = 4


def _moba_prompt_kernel(q_ref, k_ref, v_ref, km_ref, o_ref,
                        kpos_ref, qbd_ref, bias_ref, s0_ref, s1_ref, m_ref, l_ref, acc_ref):
    b = pl.program_id(0)
    i = pl.program_id(1)

    @pl.when((b == 0) & (i == 0))
    def _():
        key = lax.broadcasted_iota(jnp.int32, (MOBA_BLOCK, HEAD_PAIR), 0)
        col = lax.broadcasted_iota(jnp.int32, (MOBA_BLOCK, HEAD_PAIR), 1)
        kpos_ref[...] = jnp.where(col < SLOPE_TERMS, key, 0).astype(bf16)
        r = lax.broadcasted_iota(jnp.int32, (HEAD_PAIR, PAIR_COLS), 0)
        e_of_col = lax.broadcasted_iota(jnp.int32, (HEAD_PAIR, PAIR_COLS), 1) >> _log2(MOBA_BLOCK)
        for p in range(N_HEAD_PAIRS):
            slope_rows = jnp.zeros((HEAD_PAIR, PAIR_COLS), f32)
            for e in range(2):
                for t, term in enumerate(_bf16_terms(SLOPES[2 * p + e] * LOG2E, SLOPE_TERMS)):
                    slope_rows = jnp.where((r == t) & (e_of_col == e), term, slope_rows)
            qbd_ref[p, HEAD_PAIR:, :] = slope_rows.astype(bf16)

    feat = lax.broadcasted_iota(jnp.int32, (HEAD_PAIR, PAIR_COLS), 0)
    qcol = lax.broadcasted_iota(jnp.int32, (HEAD_PAIR, PAIR_COLS), 1)
    mine = (feat >> _log2(B_HEAD_DIM)) == (qcol >> _log2(MOBA_BLOCK))
    blk = lax.broadcasted_iota(jnp.int32, (N_BLOCKS, PAIR_COLS), 0)
    bcol = lax.broadcasted_iota(jnp.int32, (N_BLOCKS, PAIR_COLS), 1)
    blocks_back = (i - blk).astype(f32) * float(MOBA_BLOCK)
    for p in range(N_HEAD_PAIRS):
        ls = slice(p * HEAD_PAIR, (p + 1) * HEAD_PAIR)
        q2 = q_ref[0, ls, :]
        qmf = jnp.where(mine, jnp.concatenate([q2, q2], axis=1), 0.0)
        qbd_ref[p, :HEAD_PAIR, :] = (qmf * (LOG2E * B_HEAD_DIM ** -0.5)).astype(bf16)
        gate = jnp.dot(km_ref[0, :, ls], qmf, precision=lax.Precision.HIGHEST,
                       preferred_element_type=f32)
        sel = _select_blocks(gate, i, N_BLOCKS - 1, 0)
        slope = jnp.where(bcol < MOBA_BLOCK, SLOPES[2 * p] * LOG2E, SLOPES[2 * p + 1] * LOG2E)
        bias = jnp.where(sel, -slope * blocks_back, NEG)
        bias_ref[:, p * PAIR_COLS:(p + 1) * PAIR_COLS] = jnp.where(blk == i, 0.0, bias)
    m_ref[...] = jnp.full((1, ALL_COLS), NEG, f32)
    l_ref[...] = jnp.zeros((1, ALL_COLS), f32)
    acc_ref[...] = jnp.zeros((B_HEADS, B_HEAD_DIM, MOBA_BLOCK), f32)

    def scores(n, s_ref, own):
        start = pl.multiple_of(n * MOBA_BLOCK, MOBA_BLOCK)
        for p in range(N_HEAD_PAIRS):
            kn = k_ref[pl.ds(start, MOBA_BLOCK), p * HEAD_PAIR:(p + 1) * HEAD_PAIR]
            kn = jnp.concatenate([kn, kpos_ref[...]], axis=1)
            s = jnp.dot(kn, qbd_ref[p], preferred_element_type=f32)
            if own:
                key = lax.broadcasted_iota(jnp.int32, (MOBA_BLOCK, PAIR_COLS), 0)
                query = lax.broadcasted_iota(jnp.int32, (MOBA_BLOCK, PAIR_COLS), 1) & (MOBA_BLOCK - 1)
                s = jnp.where(key <= query, s, NEG)
            s_ref[p] = s

    def absorb(n, s_ref):
        start = pl.multiple_of(n * MOBA_BLOCK, MOBA_BLOCK)
        for p in range(N_HEAD_PAIRS):
            cs = slice(p * PAIR_COLS, (p + 1) * PAIR_COLS)
            s = s_ref[p]
            brow = bias_ref[pl.ds(n, 1), cs]
            m_old = m_ref[:, cs]
            m_new = jnp.maximum(m_old, jnp.max(s, axis=0, keepdims=True) + brow)
            alpha = jnp.exp2(m_old - m_new)
            pe = jnp.exp2(s - (m_new - brow))
            l_ref[:, cs] = alpha * l_ref[:, cs] + jnp.sum(pe, axis=0, keepdims=True)
            m_ref[:, cs] = m_new
            peb = pe.astype(bf16)
            for e in range(2):
                h = 2 * p + e
                es = slice(e * MOBA_BLOCK, (e + 1) * MOBA_BLOCK)
                vn = v_ref[0, h * B_HEAD_DIM:(h + 1) * B_HEAD_DIM, pl.ds(start, MOBA_BLOCK)]
                acc_ref[h] = alpha[:, es] * acc_ref[h] + jnp.dot(
                    vn, peb[:, es], preferred_element_type=f32)

    scores(i, s0_ref, True)

    def two_steps(jj, carry):
        j = 2 * jj
        scores(j, s1_ref, False)
        absorb(jnp.where(jj == 0, i, j - 1), s0_ref)
        scores(j + 1, s0_ref, False)
        absorb(j, s1_ref)
        return carry

    lax.fori_loop(0, i // 2, two_steps, 0)

    @pl.when(i % 2 == 0)
    def _():
        absorb(jnp.maximum(i - 1, 0), s0_ref)

    @pl.when(i % 2 == 1)
    def _():
        scores(i - 1, s1_ref, False)
        absorb(jnp.where(i == 1, i, i - 2), s0_ref)
        absorb(i - 1, s1_ref)

    for h in range(B_HEADS):
        o_ref[0, h * B_HEAD_DIM:(h + 1) * B_HEAD_DIM, :] = (
            acc_ref[h] / l_ref[:, h * MOBA_BLOCK:(h + 1) * MOBA_BLOCK]).astype(bf16)


def _moba_prompt(q_t, kb, vtb, kmean):
    qspec = pl.BlockSpec((1, B_WIDTH, MOBA_BLOCK), lambda b, i: (b, 0, i))
    return pl.pallas_call(
        _moba_prompt_kernel,
        grid=(BATCH, N_BLOCKS),
        in_specs=[qspec,
                  pl.BlockSpec((SEQ, B_WIDTH), lambda b, i: (b, 0)),
                  pl.BlockSpec((1, B_WIDTH, SEQ), lambda b, i: (b, 0, 0)),
                  pl.BlockSpec((1, N_BLOCKS, B_WIDTH), lambda b, i: (b, 0, 0))],
        out_specs=qspec,
        out_shape=jax.ShapeDtypeStruct((BATCH, B_WIDTH, SEQ), bf16),
        scratch_shapes=[pltpu.VMEM((MOBA_BLOCK, HEAD_PAIR), bf16),
                        pltpu.VMEM((N_HEAD_PAIRS, 2 * HEAD_PAIR, PAIR_COLS), bf16),
                        pltpu.VMEM((N_BLOCKS, ALL_COLS), f32),
                        pltpu.VMEM((N_HEAD_PAIRS, MOBA_BLOCK, PAIR_COLS), f32),
                        pltpu.VMEM((N_HEAD_PAIRS, MOBA_BLOCK, PAIR_COLS), f32),
                        pltpu.VMEM((1, ALL_COLS), f32),
                        pltpu.VMEM((1, ALL_COLS), f32),
                        pltpu.VMEM((B_HEADS, B_HEAD_DIM, MOBA_BLOCK), f32)],
        compiler_params=_params(("arbitrary", "arbitrary")),
        name="moba_prompt",
    )(q_t, kb, vtb, kmean)


SEQS_PER_STEP = 2


def _moba_sample_kernel(pt_ref, q_ref, kn_ref, vn_ref, *refs):
    seqs = range(SEQS_PER_STEP)
    k_refs = [refs[2 * u * N_PAGES:(2 * u + 1) * N_PAGES] for u in seqs]
    v_refs = [refs[(2 * u + 1) * N_PAGES:(2 * u + 2) * N_PAGES] for u in seqs]
    o_ref = refs[2 * SEQS_PER_STEP * N_PAGES]
    rows = DEC_SEQ * B_HEADS
    row = lax.broadcasted_iota(jnp.int32, (rows, B_WIDTH), 0)
    lane = lax.broadcasted_iota(jnp.int32, (rows, B_WIDTH), 1)
    head_lanes = (lane >> _log2(B_HEAD_DIM)) == (row & (B_HEADS - 1))
    r1 = lax.broadcasted_iota(jnp.int32, (rows, 1), 0)
    hrow = r1 & (B_HEADS - 1)
    slope = jnp.zeros((rows, 1), f32)
    for h in range(B_HEADS):
        slope = jnp.where(hrow == h, SLOPES[h], slope)
    tq = r1 >> _log2(B_HEADS)
    kcol = lax.broadcasted_iota(jnp.int32, (rows, PAGE_SIZE), 1)
    tk = lax.broadcasted_iota(jnp.int32, (rows, NEW_PAD), 1)
    flane = lax.broadcasted_iota(jnp.int32, (B_WIDTH, PAGE_SIZE), 1)

    qbd, raw, raw_new = [], [], []
    for u in seqs:
        qf = jnp.where(head_lanes, q_ref[u], 0.0)
        qb = (qf * (B_HEAD_DIM ** -0.5)).astype(bf16)
        qbd.append(qf)
        raw.append([jnp.dot(qb, k_refs[u][pg][...].astype(bf16), preferred_element_type=f32)
                    for pg in range(N_PAGES)])
        raw_new.append(lax.dot_general(qb, kn_ref[u].astype(bf16), _NT,
                                       preferred_element_type=f32))

    kmean = []
    for u in seqs:
        km = jnp.zeros((B_WIDTH, PAGE_SIZE), f32)
        for blk in range(N_PAST_BLOCKS):
            tot = k_refs[u][blk * PAGES_PER_BLOCK][...]
            for pg in range(blk * PAGES_PER_BLOCK + 1, (blk + 1) * PAGES_PER_BLOCK):
                tot = tot + k_refs[u][pg][...]
            mean = jnp.sum(tot, axis=1, keepdims=True) * (1.0 / MOBA_BLOCK)
            km = jnp.where(flane == blk, mean, km)
        kmean.append(km)
    gate = [jnp.dot(qbd[u], kmean[u], precision=lax.Precision.HIGHEST,
                    preferred_element_type=f32) for u in seqs]

    scores, s_new, m = [], [], []
    for u in seqs:
        sel = _select_blocks(gate[u], N_PAST_BLOCKS, N_PAST_BLOCKS, 1)
        blk_bias = jnp.where(sel, 0.0, NEG)
        sn = jnp.where(tk <= tq, raw_new[u] - slope * (tq - tk).astype(f32), NEG)
        mu = jnp.max(sn, axis=1, keepdims=True)
        su = []
        for pg in range(N_PAGES):
            dist = (tq + (PAST_LEN - pg * PAGE_SIZE) - kcol).astype(f32)
            blk = pg // PAGES_PER_BLOCK
            s = raw[u][pg] - slope * dist + blk_bias[:, blk:blk + 1]
            mu = jnp.maximum(mu, jnp.max(s, axis=1, keepdims=True))
            su.append(s)
        scores.append(su)
        s_new.append(sn)
        m.append(mu)

    for u in seqs:
        p_new = jnp.exp(s_new[u] - m[u])
        l = jnp.sum(p_new, axis=1, keepdims=True)
        acc = jnp.dot(p_new.astype(bf16), vn_ref[u].astype(bf16), preferred_element_type=f32)
        for pg in range(N_PAGES):
            pe = jnp.exp(scores[u][pg] - m[u])
            l = l + jnp.sum(pe, axis=1, keepdims=True)
            acc = acc + lax.dot_general(pe.astype(bf16), v_refs[u][pg][...].astype(bf16), _NT,
                                        preferred_element_type=f32)
        out = jnp.where(head_lanes, acc / l, 0.0)
        o_ref[u] = jnp.sum(out.reshape(DEC_SEQ, B_HEADS, B_WIDTH), axis=1)


def _moba_sample(page_table, q, k_new, v_new, cache_k, cache_v, layer):
    seq_spec = lambda r: pl.BlockSpec((SEQS_PER_STEP, r, B_WIDTH), lambda b, pt: (b, 0, 0))

    def page_spec(u, pg):
        return pl.BlockSpec(
            (None, None, B_WIDTH, PAGE_SIZE),
            lambda b, pt: (layer, pt[(b * SEQS_PER_STEP + u) * N_PAGES + pg], 0, 0))

    pages, operands = [], []
    for u in range(SEQS_PER_STEP):
        for cache in (cache_k, cache_v):
            pages += [page_spec(u, pg) for pg in range(N_PAGES)]
            operands += [cache] * N_PAGES
    return pl.pallas_call(
        _moba_sample_kernel,
        grid_spec=pltpu.PrefetchScalarGridSpec(
            num_scalar_prefetch=1,
            grid=(DEC_BATCH // SEQS_PER_STEP,),
            in_specs=[seq_spec(DEC_SEQ * B_HEADS), seq_spec(NEW_PAD), seq_spec(NEW_PAD)] + pages,
            out_specs=seq_spec(DEC_SEQ)),
        out_shape=jax.ShapeDtypeStruct((DEC_BATCH, DEC_SEQ, B_WIDTH), f32),
        compiler_params=_params(("parallel",)),
        name="moba_sample",
    )(page_table, q, k_new, v_new, *operands)


def _even_out_kernel(x_ref, ya_ref, ybt_ref, ybs_ref, w_ref, o_ref):
    i = pl.program_id(0)
    is_prompt = i < N_PROMPT_TILES
    base = x_ref[...] + jnp.dot(ya_ref[...], w_ref[:A_WIDTH, :], preferred_element_type=f32)

    @pl.when(is_prompt)
    def _():
        o_ref[...] = base + lax.dot_general(ybt_ref[0], w_ref[A_WIDTH:, :], _TN,
                                            preferred_element_type=f32)

    @pl.when(jnp.logical_not(is_prompt))
    def _():
        o_ref[...] = base + jnp.dot(ybs_ref[...], w_ref[A_WIDTH:, :], preferred_element_type=f32)


def _even_out(x, ya, yb_t, yb_s, w_out):
    return pl.pallas_call(
        _even_out_kernel,
        grid=(N_TILES,),
        in_specs=[_tile_spec(D_MODEL), _tile_spec(A_WIDTH), _feature_major_spec(),
                  _const_spec((TM, B_WIDTH)), _const_spec((D_MODEL, D_MODEL))],
        out_specs=_tile_spec(D_MODEL),
        out_shape=jax.ShapeDtypeStruct((N_TOK, D_MODEL), f32),
        compiler_params=_params(("arbitrary",)),
        name="even_out",
    )(x, ya, yb_t, yb_s, w_out)


def _odd_kernel(x_ref, g_ref, wi_ref, cw_ref, wo_ref, st_ref,
                o_ref, cp_ref, cs_ref, z_ref, ext_ref):
    i = pl.program_id(0)
    is_prompt = i < N_PROMPT_TILES
    h = _rms(x_ref[...], g_ref[...]).astype(bf16)
    z_ref[...] = jnp.dot(h, wi_ref[...], preferred_element_type=f32)

    @pl.when(i % TILES_PER_SEQ == 0)
    def _():
        ext_ref[CONV_FRONT - SUBLANES:CONV_FRONT, :] = jnp.zeros((SUBLANES, C_WIDTH), f32)

    @pl.when(jnp.logical_not(is_prompt))
    def _():
        ext_ref[:CONV_FRONT, :] = st_ref[...]

    cx = z_ref[:, 2 * C_WIDTH:] * z_ref[:, :C_WIDTH]
    ext_ref[CONV_FRONT:, :] = cx
    back1 = jnp.where(is_prompt, ext_ref[CONV_FRONT - 1:CONV_FRONT - 1 + TM, :],
                      ext_ref[CONV_FRONT - DEC_BATCH:CONV_FRONT - DEC_BATCH + TM, :])
    back2 = jnp.where(is_prompt, ext_ref[CONV_FRONT - 2:CONV_FRONT - 2 + TM, :],
                      ext_ref[:TM, :])
    conv = cw_ref[0:1, :] * back2 + cw_ref[1:2, :] * back1 + cw_ref[2:3, :] * cx
    gated = (z_ref[:, C_WIDTH:2 * C_WIDTH] * conv).astype(bf16)
    o_ref[...] = x_ref[...] + jnp.dot(gated, wo_ref[...], preferred_element_type=f32)

    @pl.when(is_prompt)
    def _():
        last = ext_ref[CONV_FRONT + TM - SUBLANES:, :]
        ext_ref[CONV_FRONT - SUBLANES:CONV_FRONT, :] = last
        cp_ref[0] = last

    @pl.when(i == N_TILES - 1)
    def _():
        cs_ref[...] = ext_ref[TM:, :]


def _odd(x, g, w_in, conv_w, w_out, state):
    return pl.pallas_call(
        _odd_kernel,
        grid=(N_TILES,),
        in_specs=[_tile_spec(D_MODEL), _const_spec((1, D_MODEL)),
                  _const_spec((D_MODEL, 3 * C_WIDTH)), _const_spec((CONV_W, C_WIDTH)),
                  _const_spec((C_WIDTH, D_MODEL)), _const_spec((CONV_FRONT, C_WIDTH))],
        out_specs=[_tile_spec(D_MODEL),
                   pl.BlockSpec((1, SUBLANES, C_WIDTH),
                                lambda i: (jnp.minimum(i // TILES_PER_SEQ, BATCH - 1), 0, 0)),
                   pl.BlockSpec((CONV_FRONT, C_WIDTH), lambda i: (0, 0))],
        out_shape=[jax.ShapeDtypeStruct((N_TOK, D_MODEL), f32),
                   jax.ShapeDtypeStruct((BATCH, SUBLANES, C_WIDTH), f32),
                   jax.ShapeDtypeStruct((CONV_FRONT, C_WIDTH), f32)],
        scratch_shapes=[pltpu.VMEM((TM, 3 * C_WIDTH), f32),
                        pltpu.VMEM((CONV_FRONT + TM, C_WIDTH), f32)],
        compiler_params=_params(("arbitrary",)),
        name="odd_mixer",
    )(x, g, w_in, conv_w, w_out, state)


def _step_major(a):
    return jnp.swapaxes(a, 0, 1).reshape(a.shape[0] * a.shape[1], a.shape[2])


def _batch_major(a, steps=DEC_SEQ):
    return jnp.swapaxes(a.reshape(steps, DEC_BATCH, a.shape[-1]), 0, 1)


def kernel(x_prompt, x_sample, cache_k, cache_v, state_conv, page_table, ffn1_norm, ffn1_gate, ffn1_up, ffn1_down, mix_norm, ffn2_norm, ffn2_gate, ffn2_up, ffn2_down, even_w_in, even_w_out, a_v_gain, a_w_s, a_b_s, odd_w_in, odd_conv_w, odd_w_out, final_norm):
    n_phys = cache_k.shape[1]
    pages = lambda c: jnp.transpose(c, (0, 1, 3, 4, 2)).reshape(N_EVEN, n_phys, B_WIDTH, PAGE_SIZE)
    ck, cv = pages(cache_k), pages(cache_v)
    pt = page_table.reshape(-1).astype(jnp.int32)
    row = lambda a: a.reshape(1, -1)
    fg = row(final_norm)
    xs = (x_prompt.reshape(N_PROMPT, D_MODEL), _step_major(x_sample))

    k_p, v_p, k_s, v_s, conv_p, conv_s, av_s = [], [], [], [], [], [], []
    for i in range(DEPTH):
        j = i // 2
        (x,) = _ffn(xs, row(ffn1_norm[i]), ffn1_gate, ffn1_up, ffn1_down, i, fg, first=(i == 0))
        if i % 2 == 0:
            reps = TM // A_CHUNK
            w4 = a_w_s[j][:, :DEC_SEQ, :DEC_SEQ]
            ws_tiled = jnp.stack([
                jnp.tile(a_w_s[j], (1, reps, reps)),
                jnp.repeat(jnp.repeat(w4, DEC_BATCH, axis=1), DEC_BATCH, axis=2)])
            bs_tiled = jnp.stack([
                jnp.tile(a_b_s[j], (1, reps)),
                jnp.repeat(a_b_s[j][:, :DEC_SEQ], DEC_BATCH, axis=1)])[..., None]
            w_in = even_w_in[j].astype(bf16)
            w_row = jnp.concatenate([w_in[:, :2 * A_WIDTH],
                                     w_in[:, 2 * A_WIDTH + B_WIDTH:2 * A_WIDTH + 2 * B_WIDTH]], axis=1)
            w_t = w_in[:, 2 * A_WIDTH:].T
            ya, va, q_t, k_t, v_t, vtb, kb, km, st = _even_in(
                x, row(mix_norm[i]), w_row, w_t, row(a_v_gain[j]), ws_tiled, bs_tiled)
            kmean = km.reshape(N_TOK // MOBA_BLOCK, B_WIDTH)[:BATCH * N_BLOCKS]
            yb_t = _moba_prompt(q_t, kb, vtb, kmean.reshape(BATCH, N_BLOCKS, B_WIDTH))
            per_seq = jnp.transpose(st, (0, 3, 1, 2))
            padn = lambda a: jnp.pad(a, ((0, 0), (0, NEW_PAD - DEC_SEQ), (0, 0)))
            q_rep = jnp.repeat(per_seq[0], B_HEADS, axis=1)
            yb_s = _moba_sample(pt, q_rep, padn(per_seq[1]), padn(per_seq[2]), ck, cv, j)
            x = _even_out(x, ya, yb_t, _step_major(yb_s).astype(bf16), even_w_out[j].astype(bf16))
            k_p.append(k_t.reshape(BATCH, B_HEADS, B_HEAD_DIM, SEQ))
            v_p.append(v_t.reshape(BATCH, B_HEADS, B_HEAD_DIM, SEQ))
            k_s.append(st[1].reshape(DEC_SEQ, B_HEADS, B_HEAD_DIM, DEC_BATCH))
            v_s.append(st[2].reshape(DEC_SEQ, B_HEADS, B_HEAD_DIM, DEC_BATCH))
            av_s.append(_batch_major(va))
        else:
            state = jnp.swapaxes(state_conv[j], 0, 1).reshape(CONV_FRONT, C_WIDTH)
            x, cp, cs = _odd(x, row(mix_norm[i]), odd_w_in[j].astype(bf16), odd_conv_w[j],
                             odd_w_out[j].astype(bf16), state)
            conv_p.append(cp[:, SUBLANES - (CONV_W - 1):])
            conv_s.append(_batch_major(cs, CONV_W - 1))
        xs = _ffn((x,), row(ffn2_norm[i]), ffn2_gate, ffn2_up, ffn2_down, i, fg,
                  final=(i == DEPTH - 1))

    y_prompt = xs[0].reshape(BATCH, SEQ, D_MODEL)
    y_sample = _batch_major(xs[1])
    to_rows_p = lambda ts: jnp.transpose(jnp.stack(ts), (0, 1, 4, 2, 3))
    to_rows_s = lambda ts: jnp.transpose(jnp.stack(ts), (0, 4, 1, 2, 3))
    return (y_prompt, y_sample, to_rows_p(k_p), to_rows_p(v_p), jnp.stack(conv_p),
            to_rows_s(k_s), to_rows_s(v_s), jnp.stack(conv_s), jnp.stack(av_s))
```

```python
import functools
import struct

import jax
import jax.numpy as jnp
from jax import lax
from jax.experimental import pallas as pl
from jax.experimental.pallas import tpu as pltpu

D_MODEL = 1024
BATCH = 4
SEQ = 4096
DEPTH = 4
DEC_BATCH = 128
DEC_SEQ = 4
PAST_LEN = 2048
PAGE_SIZE = 128
N_EVEN = (DEPTH + 1) // 2
N_ODD = DEPTH // 2
D_FF = 2816
RMS_EPS = 1e-6
A_WIDTH = D_MODEL // 2
A_CHUNK = 128
A_GROUPS = 4
A_GROUP_DIM = A_WIDTH // A_GROUPS
B_HEAD_DIM = 64
B_WIDTH = D_MODEL - A_WIDTH
B_HEADS = B_WIDTH // B_HEAD_DIM
MOBA_BLOCK = 256
MOBA_TOPK = 3
C_WIDTH = D_MODEL
CONV_W = 3

N_PROMPT = BATCH * SEQ
N_SAMPLE = DEC_BATCH * DEC_SEQ
N_TOK = N_PROMPT + N_SAMPLE
TM = 512
N_TILES = N_TOK // TM
N_PROMPT_TILES = N_PROMPT // TM
TILES_PER_SEQ = SEQ // TM
FF_CHUNK = 256
N_FF_CHUNKS = D_FF // FF_CHUNK
N_PAGES = PAST_LEN // PAGE_SIZE
N_BLOCKS = SEQ // MOBA_BLOCK
N_PAST_BLOCKS = PAST_LEN // MOBA_BLOCK
PAGES_PER_BLOCK = MOBA_BLOCK // PAGE_SIZE
BLOCKS_PER_TILE = TM // MOBA_BLOCK
NEW_PAD = 16
HEAD_PAIR = 2 * B_HEAD_DIM
N_HEAD_PAIRS = B_HEADS // 2
SLOPES = tuple(2.0 ** (-8.0 * (h + 1) / B_HEADS) for h in range(B_HEADS))
NEG = -1e30
SUBLANES = 8
LANES = 128
CONV_FRONT = 2 * DEC_BATCH
CONV_CHUNK = 256
VMEM_LIMIT = 56 * 1024 * 1024

assert N_SAMPLE == TM and N_TOK % TM == 0 and D_FF % FF_CHUNK == 0
assert DEC_BATCH == LANES and CONV_W - 1 == 2 and DEC_SEQ >= CONV_W - 1

_NT = (((1,), (1,)), ((), ()))
_TN = (((0,), (0,)), ((), ()))
bf16 = jnp.bfloat16
f32 = jnp.float32


def _log2(n):
    assert n & (n - 1) == 0
    return n.bit_length() - 1


def _rms(x, g):
    ms = jnp.mean(x * x, axis=-1, keepdims=True)
    return x * lax.rsqrt(ms + RMS_EPS) * g


def _params(sem):
    return pltpu.CompilerParams(dimension_semantics=sem, vmem_limit_bytes=VMEM_LIMIT)


def _const_spec(shape):
    nd = len(shape)
    return pl.BlockSpec(shape, lambda *_: (0,) * nd, pipeline_mode=pl.Buffered(1))


def _tile_spec(width):
    return pl.BlockSpec((TM, width), lambda i: (i, 0))


def _prompt_tile_spec(width):
    return pl.BlockSpec((TM, width), lambda i: (jnp.minimum(i, N_PROMPT_TILES - 1), 0))


def _feature_major_spec():
    return pl.BlockSpec((B_WIDTH, TM), lambda i: (0, i))


def _prompt_feature_major_spec():
    return pl.BlockSpec((B_WIDTH, TM), lambda i: (0, jnp.minimum(i, N_PROMPT_TILES - 1)))


def _ffn_kernel(*refs, first, final):
    n_x = 2 if first else 1
    x_refs, (g_ref, wg_ref, wu_ref, wd_ref, fg_ref) = refs[:n_x], refs[n_x:n_x + 5]
    o_refs, acc_ref = refs[n_x + 5:-1], refs[-1]
    i = pl.program_id(0)
    is_prompt = i < N_PROMPT_TILES
    if first:
        x = jnp.where(is_prompt, x_refs[0][...], x_refs[1][...])
    else:
        x = x_refs[0][...]
    h = _rms(x, g_ref[...]).astype(bf16)
    for c in range(N_FF_CHUNKS):
        sl = slice(c * FF_CHUNK, (c + 1) * FF_CHUNK)
        gate = jnp.dot(h, wg_ref[:, sl].astype(bf16), preferred_element_type=f32)
        up = jnp.dot(h, wu_ref[:, sl].astype(bf16), preferred_element_type=f32)
        act = (gate / (1.0 + jnp.exp(-gate)) * up).astype(bf16)
        down = jnp.dot(act, wd_ref[sl, :].astype(bf16), preferred_element_type=f32)
        if c == 0:
            acc_ref[...] = down
        else:
            acc_ref[...] += down
    y = x + 0.5 * acc_ref[...]
    if final:
        y = _rms(y, fg_ref[...])

        @pl.when(is_prompt)
        def _():
            o_refs[0][...] = y

        @pl.when(jnp.logical_not(is_prompt))
        def _():
            o_refs[1][...] = y
    else:
        o_refs[0][...] = y


def _ffn(xs, g, wg, wu, wd, layer, fg, first=False, final=False):
    def layer_spec(rows, cols):
        return pl.BlockSpec((None, rows, cols), lambda i: (layer, 0, 0),
                            pipeline_mode=pl.Buffered(1))

    sample_spec = pl.BlockSpec((TM, D_MODEL), lambda i: (0, 0))
    x_specs = [_prompt_tile_spec(D_MODEL), sample_spec] if first else [_tile_spec(D_MODEL)]
    if final:
        out_specs = [_prompt_tile_spec(D_MODEL), sample_spec]
        out_shape = [jax.ShapeDtypeStruct((N_PROMPT, D_MODEL), f32),
                     jax.ShapeDtypeStruct((N_SAMPLE, D_MODEL), f32)]
    else:
        out_specs = [_tile_spec(D_MODEL)]
        out_shape = [jax.ShapeDtypeStruct((N_TOK, D_MODEL), f32)]
    return pl.pallas_call(
        functools.partial(_ffn_kernel, first=first, final=final),
        grid=(N_TILES,),
        in_specs=x_specs + [_const_spec((1, D_MODEL)),
                            layer_spec(D_MODEL, D_FF), layer_spec(D_MODEL, D_FF),
                            layer_spec(D_FF, D_MODEL), _const_spec((1, D_MODEL))],
        out_specs=out_specs,
        out_shape=out_shape,
        scratch_shapes=[pltpu.VMEM((TM, D_MODEL), f32)],
        compiler_params=_params(("arbitrary",)),
        name="ffn",
    )(*xs, g, wg, wu, wd, fg)


ROW_COLS = 2 * A_WIDTH + B_WIDTH
T_ROWS = 3 * B_WIDTH


def _even_in_kernel(x_ref, g_ref, w_ref, wt_ref, gain_ref, ws_ref, bs_ref,
                    ya_ref, vs_ref, qt_ref, kt_ref, vt_ref, vtb_ref, kb_ref, km_ref):
    i = pl.program_id(0)
    is_prompt = i < N_PROMPT_TILES
    h = _rms(x_ref[...], g_ref[...]).astype(bf16)
    project = lambda cols: jnp.dot(h, w_ref[:, cols], preferred_element_type=f32)

    def project_t(part):
        rows = slice(part * B_WIDTH, (part + 1) * B_WIDTH)
        return lax.dot_general(wt_ref[rows, :], h, _NT, preferred_element_type=f32)

    def gelu(a):
        return a * (0.5 * (1.0 + jnp.tanh(0.7978845608028654 * (a + 0.044715 * (a * a * a)))))

    row = lax.broadcasted_iota(jnp.int32, (TM, TM), 0)
    col = lax.broadcasted_iota(jnp.int32, (TM, TM), 1)
    chunk = lambda r: jnp.where(is_prompt, r >> _log2(A_CHUNK), r & (DEC_BATCH - 1))
    keep = (chunk(row) == chunk(col)) & (col <= row)
    feature_major = (qt_ref, kt_ref)
    pair = 2 * A_GROUP_DIM
    for gp in range(A_GROUPS // 2):
        u2 = gelu(project(slice(gp * pair, (gp + 1) * pair)))
        v2 = gelu(project(slice(A_WIDTH + gp * pair, A_WIDTH + (gp + 1) * pair)))
        feature_major[gp][...] = project_t(gp)
        for gg in range(2):
            g = 2 * gp + gg
            gs = slice(g * A_GROUP_DIM, (g + 1) * A_GROUP_DIM)
            ps = slice(gg * A_GROUP_DIM, (gg + 1) * A_GROUP_DIM)
            vn = _rms(v2[:, ps], gain_ref[:, gs])
            w = jnp.where(keep, ws_ref[0, g], 0.0).astype(bf16)
            mixed = jnp.dot(w, vn.astype(bf16), preferred_element_type=f32) + bs_ref[0, g]
            ya_ref[:, gs] = (u2[:, ps] * mixed).astype(bf16)

            @pl.when(i == N_TILES - 1)
            def _():
                vs_ref[:, gs] = vn

    k = project(slice(2 * A_WIDTH, ROW_COLS))
    kb_ref[...] = k.astype(bf16)
    for blk in range(BLOCKS_PER_TILE):
        ks = jnp.sum(k[blk * MOBA_BLOCK:(blk + 1) * MOBA_BLOCK], axis=0, keepdims=True)
        km_ref[0, blk:blk + 1, :] = ks * (1.0 / MOBA_BLOCK)
    vt = project_t(2)
    vt_ref[...] = vt
    vtb_ref[...] = vt.astype(bf16)


def _even_in(x, g, w_row, w_t, gain, ws_tiled, bs_tiled):
    kind = lambda i: (i // N_PROMPT_TILES, 0, 0, 0)
    fm = lambda dt: jax.ShapeDtypeStruct((B_WIDTH, N_TOK), dt)
    return pl.pallas_call(
        _even_in_kernel,
        grid=(N_TILES,),
        in_specs=[_tile_spec(D_MODEL), _const_spec((1, D_MODEL)),
                  _const_spec((D_MODEL, ROW_COLS)), _const_spec((T_ROWS, D_MODEL)),
                  _const_spec((1, A_WIDTH)),
                  pl.BlockSpec((1, A_GROUPS, TM, TM), kind),
                  pl.BlockSpec((1, A_GROUPS, TM, 1), kind)],
        out_specs=[_tile_spec(A_WIDTH),
                   pl.BlockSpec((TM, A_WIDTH), lambda i: (0, 0)),
                   _feature_major_spec(), _feature_major_spec(), _feature_major_spec(),
                   _feature_major_spec(),
                   _tile_spec(B_WIDTH),
                   pl.BlockSpec((1, BLOCKS_PER_TILE, B_WIDTH), lambda i: (i, 0, 0))],
        out_shape=[jax.ShapeDtypeStruct((N_TOK, A_WIDTH), bf16),
                   jax.ShapeDtypeStruct((N_SAMPLE, A_WIDTH), f32),
                   fm(f32), fm(f32), fm(f32), fm(bf16),
                   jax.ShapeDtypeStruct((N_TOK, B_WIDTH), bf16),
                   jax.ShapeDtypeStruct((N_TILES, BLOCKS_PER_TILE, B_WIDTH), f32)],
        compiler_params=_params(("arbitrary",)),
        name="even_in",
    )(x, g, w_row, w_t, gain, ws_tiled, bs_tiled)


def _select_blocks(gate, n_valid, n_past, axis):
    blk = lax.broadcasted_iota(jnp.int32, gate.shape, axis)
    valid = blk < n_valid
    g = jnp.where(valid, gate, -jnp.inf)
    sel = jnp.zeros(gate.shape, jnp.bool_)
    for _ in range(min(MOBA_TOPK, n_past)):
        mx = jnp.max(g, axis=axis, keepdims=True)
        idx = jnp.min(jnp.where(g == mx, blk, n_past), axis=axis, keepdims=True)
        hit = blk == idx
        sel = sel | (hit & valid)
        g = jnp.where(hit, -jnp.inf, g)
    return sel


PAIR_COLS = 2 * MOBA_BLOCK
ALL_COLS = B_HEADS * MOBA_BLOCK


LOG2E = 1.4426950408889634


def _round_bf16(x):
    bits = struct.unpack("<I", struct.pack("<f", x))[0]
    bits = (bits + 0x7FFF + ((bits >> 16) & 1)) & 0xFFFF0000
    return struct.unpack("<f", struct.pack("<I", bits))[0]


def _bf16_terms(x, n):
    terms = []
    for _ in range(n):
        t = _round_bf16(x)
        terms.append(t)
        x -= t
    return terms


SLOPE_TERMS = 4
SUM_ROWS = 16


def _moba_prompt_kernel(q_ref, k_ref, v_ref, km_ref, o_ref,
                        kpos_ref, qbd_ref, bias_ref, s0_ref, s1_ref, m_ref, acc_ref):
    b = pl.program_id(0)
    i = pl.program_id(1)

    @pl.when((b == 0) & (i == 0))
    def _():
        key = lax.broadcasted_iota(jnp.int32, (MOBA_BLOCK, HEAD_PAIR), 0)
        col = lax.broadcasted_iota(jnp.int32, (MOBA_BLOCK, HEAD_PAIR), 1)
        kpos_ref[...] = jnp.where(col < SLOPE_TERMS, key, 0).astype(bf16)
        r = lax.broadcasted_iota(jnp.int32, (HEAD_PAIR, PAIR_COLS), 0)
        e_of_col = lax.broadcasted_iota(jnp.int32, (HEAD_PAIR, PAIR_COLS), 1) >> _log2(MOBA_BLOCK)
        for p in range(N_HEAD_PAIRS):
            slope_rows = jnp.zeros((HEAD_PAIR, PAIR_COLS), f32)
            for e in range(2):
                for t, term in enumerate(_bf16_terms(SLOPES[2 * p + e] * LOG2E, SLOPE_TERMS)):
                    slope_rows = jnp.where((r == t) & (e_of_col == e), term, slope_rows)
            qbd_ref[p, HEAD_PAIR:, :] = slope_rows.astype(bf16)

    feat = lax.broadcasted_iota(jnp.int32, (HEAD_PAIR, PAIR_COLS), 0)
    qcol = lax.broadcasted_iota(jnp.int32, (HEAD_PAIR, PAIR_COLS), 1)
    mine = (feat >> _log2(B_HEAD_DIM)) == (qcol >> _log2(MOBA_BLOCK))
    blk = lax.broadcasted_iota(jnp.int32, (N_BLOCKS, PAIR_COLS), 0)
    bcol = lax.broadcasted_iota(jnp.int32, (N_BLOCKS, PAIR_COLS), 1)
    blocks_back = (i - blk).astype(f32) * float(MOBA_BLOCK)
    for p in range(N_HEAD_PAIRS):
        ls = slice(p * HEAD_PAIR, (p + 1) * HEAD_PAIR)
        q2 = q_ref[ls, :]
        qmf = jnp.where(mine, jnp.concatenate([q2, q2], axis=1), 0.0)
        qbd_ref[p, :HEAD_PAIR, :] = (qmf * (LOG2E * B_HEAD_DIM ** -0.5)).astype(bf16)
        gate = jnp.dot(km_ref[0, :, ls], qmf, precision=lax.Precision.HIGHEST,
                       preferred_element_type=f32)
        sel = _select_blocks(gate, i, N_BLOCKS - 1, 0)
        slope = jnp.where(bcol < MOBA_BLOCK, SLOPES[2 * p] * LOG2E, SLOPES[2 * p + 1] * LOG2E)
        bias = jnp.where(sel, -slope * blocks_back, NEG)
        bias_ref[:, p * PAIR_COLS:(p + 1) * PAIR_COLS] = jnp.where(blk == i, 0.0, bias)
    m_ref[...] = jnp.full((1, ALL_COLS), NEG, f32)
    acc_ref[...] = jnp.zeros((B_HEADS, B_HEAD_DIM + SUM_ROWS, MOBA_BLOCK), f32)

    def scores(n, s_ref, own):
        start = pl.multiple_of(n * MOBA_BLOCK, MOBA_BLOCK)
        for p in range(N_HEAD_PAIRS):
            kn = k_ref[pl.ds(start, MOBA_BLOCK), p * HEAD_PAIR:(p + 1) * HEAD_PAIR]
            kn = jnp.concatenate([kn, kpos_ref[...]], axis=1)
            s = jnp.dot(kn, qbd_ref[p], preferred_element_type=f32)
            if own:
                key = lax.broadcasted_iota(jnp.int32, (MOBA_BLOCK, PAIR_COLS), 0)
                query = lax.broadcasted_iota(jnp.int32, (MOBA_BLOCK, PAIR_COLS), 1) & (MOBA_BLOCK - 1)
                s = jnp.where(key <= query, s, NEG)
            s_ref[p] = s

    def absorb(n, s_ref):
        start = pl.multiple_of(n * MOBA_BLOCK, MOBA_BLOCK)
        for p in range(N_HEAD_PAIRS):
            cs = slice(p * PAIR_COLS, (p + 1) * PAIR_COLS)
            s = s_ref[p]
            brow = bias_ref[pl.ds(n, 1), cs]
            m_old = m_ref[:, cs]
            m_new = jnp.maximum(m_old, jnp.max(s, axis=0, keepdims=True) + brow)
            alpha = jnp.exp2(m_old - m_new)
            peb = jnp.exp2(s - (m_new - brow)).astype(bf16)
            m_ref[:, cs] = m_new
            for e in range(2):
                h = 2 * p + e
                es = slice(e * MOBA_BLOCK, (e + 1) * MOBA_BLOCK)
                vn = v_ref[h * B_HEAD_DIM:(h + 1) * B_HEAD_DIM, pl.ds(start, MOBA_BLOCK)]
                vn = jnp.concatenate([vn, jnp.ones((SUM_ROWS, MOBA_BLOCK), bf16)], axis=0)
                acc_ref[h] = alpha[:, es] * acc_ref[h] + jnp.dot(
                    vn, peb[:, es], preferred_element_type=f32)

    scores(i, s0_ref, True)

    def two_steps(jj, carry):
        j = 2 * jj
        scores(j, s1_ref, False)
        absorb(jnp.where(jj == 0, i, j - 1), s0_ref)
        scores(j + 1, s0_ref, False)
        absorb(j, s1_ref)
        return carry

    lax.fori_loop(0, i // 2, two_steps, 0)

    @pl.when(i % 2 == 0)
    def _():
        absorb(jnp.maximum(i - 1, 0), s0_ref)

    @pl.when(i % 2 == 1)
    def _():
        scores(i - 1, s1_ref, False)
        absorb(jnp.where(i == 1, i, i - 2), s0_ref)
        absorb(i - 1, s1_ref)

    for h in range(B_HEADS):
        o_ref[h * B_HEAD_DIM:(h + 1) * B_HEAD_DIM, :] = (
            acc_ref[h, :B_HEAD_DIM, :] / acc_ref[h, B_HEAD_DIM:B_HEAD_DIM + 1, :]).astype(bf16)


def _moba_prompt(q_t, kb, vtb, kmean):
    qspec = pl.BlockSpec((B_WIDTH, MOBA_BLOCK), lambda b, i: (0, b * N_BLOCKS + i))
    return pl.pallas_call(
        _moba_prompt_kernel,
        grid=(BATCH, N_BLOCKS),
        in_specs=[qspec,
                  pl.BlockSpec((SEQ, B_WIDTH), lambda b, i: (b, 0)),
                  pl.BlockSpec((B_WIDTH, SEQ), lambda b, i: (0, b)),
                  pl.BlockSpec((1, N_BLOCKS, B_WIDTH), lambda b, i: (b, 0, 0))],
        out_specs=qspec,
        out_shape=jax.ShapeDtypeStruct((B_WIDTH, N_PROMPT), bf16),
        scratch_shapes=[pltpu.VMEM((MOBA_BLOCK, HEAD_PAIR), bf16),
                        pltpu.VMEM((N_HEAD_PAIRS, 2 * HEAD_PAIR, PAIR_COLS), bf16),
                        pltpu.VMEM((N_BLOCKS, ALL_COLS), f32),
                        pltpu.VMEM((N_HEAD_PAIRS, MOBA_BLOCK, PAIR_COLS), f32),
                        pltpu.VMEM((N_HEAD_PAIRS, MOBA_BLOCK, PAIR_COLS), f32),
                        pltpu.VMEM((1, ALL_COLS), f32),
                        pltpu.VMEM((B_HEADS, B_HEAD_DIM + SUM_ROWS, MOBA_BLOCK), f32)],
        compiler_params=_params(("arbitrary", "arbitrary")),
        name="moba_prompt",
    )(q_t, kb, vtb, kmean)


SEQS_PER_STEP = 2


def _moba_sample_kernel(pt_ref, q_ref, kn_ref, vn_ref, *refs):
    seqs = range(SEQS_PER_STEP)
    k_refs = [refs[2 * u * N_PAGES:(2 * u + 1) * N_PAGES] for u in seqs]
    v_refs = [refs[(2 * u + 1) * N_PAGES:(2 * u + 2) * N_PAGES] for u in seqs]
    o_ref = refs[2 * SEQS_PER_STEP * N_PAGES]
    rows = DEC_SEQ * B_HEADS
    row = lax.broadcasted_iota(jnp.int32, (rows, B_WIDTH), 0)
    lane = lax.broadcasted_iota(jnp.int32, (rows, B_WIDTH), 1)
    head_lanes = (lane >> _log2(B_HEAD_DIM)) == (row & (B_HEADS - 1))
    r1 = lax.broadcasted_iota(jnp.int32, (rows, 1), 0)
    hrow = r1 & (B_HEADS - 1)
    slope = jnp.zeros((rows, 1), f32)
    for h in range(B_HEADS):
        slope = jnp.where(hrow == h, SLOPES[h], slope)
    tq = r1 >> _log2(B_HEADS)
    kcol = lax.broadcasted_iota(jnp.int32, (rows, PAGE_SIZE), 1)
    tk = lax.broadcasted_iota(jnp.int32, (rows, NEW_PAD), 1)
    flane = lax.broadcasted_iota(jnp.int32, (B_WIDTH, PAGE_SIZE), 1)

    qbd, raw, raw_new = [], [], []
    for u in seqs:
        qf = jnp.where(head_lanes, q_ref[u], 0.0)
        qb = (qf * (B_HEAD_DIM ** -0.5)).astype(bf16)
        qbd.append(qf)
        raw.append([jnp.dot(qb, k_refs[u][pg][...].astype(bf16), preferred_element_type=f32)
                    for pg in range(N_PAGES)])
        raw_new.append(lax.dot_general(qb, kn_ref[u].astype(bf16), _NT,
                                       preferred_element_type=f32))

    kmean = []
    for u in seqs:
        km = jnp.zeros((B_WIDTH, PAGE_SIZE), f32)
        for blk in range(N_PAST_BLOCKS):
            tot = k_refs[u][blk * PAGES_PER_BLOCK][...]
            for pg in range(blk * PAGES_PER_BLOCK + 1, (blk + 1) * PAGES_PER_BLOCK):
                tot = tot + k_refs[u][pg][...]
            mean = jnp.sum(tot, axis=1, keepdims=True) * (1.0 / MOBA_BLOCK)
            km = jnp.where(flane == blk, mean, km)
        kmean.append(km)
    gate = [jnp.dot(qbd[u], kmean[u], precision=lax.Precision.HIGHEST,
                    preferred_element_type=f32) for u in seqs]

    scores, s_new, m = [], [], []
    for u in seqs:
        sel = _select_blocks(gate[u], N_PAST_BLOCKS, N_PAST_BLOCKS, 1)
        blk_bias = jnp.where(sel, 0.0, NEG)
        sn = jnp.where(tk <= tq, raw_new[u] - slope * (tq - tk).astype(f32), NEG)
        mu = jnp.max(sn, axis=1, keepdims=True)
        su = []
        for pg in range(N_PAGES):
            dist = (tq + (PAST_LEN - pg * PAGE_SIZE) - kcol).astype(f32)
            blk = pg // PAGES_PER_BLOCK
            s = raw[u][pg] - slope * dist + blk_bias[:, blk:blk + 1]
            mu = jnp.maximum(mu, jnp.max(s, axis=1, keepdims=True))
            su.append(s)
        scores.append(su)
        s_new.append(sn)
        m.append(mu)

    for u in seqs:
        p_new = jnp.exp(s_new[u] - m[u])
        l = jnp.sum(p_new, axis=1, keepdims=True)
        acc = jnp.dot(p_new.astype(bf16), vn_ref[u].astype(bf16), preferred_element_type=f32)
        for pg in range(N_PAGES):
            pe = jnp.exp(scores[u][pg] - m[u])
            l = l + jnp.sum(pe, axis=1, keepdims=True)
            acc = acc + lax.dot_general(pe.astype(bf16), v_refs[u][pg][...].astype(bf16), _NT,
                                        preferred_element_type=f32)
        out = jnp.where(head_lanes, acc / l, 0.0)
        o_ref[u] = jnp.sum(out.reshape(DEC_SEQ, B_HEADS, B_WIDTH), axis=1)


def _moba_sample(page_table, q, k_new, v_new, cache_k, cache_v, layer):
    seq_spec = lambda r: pl.BlockSpec((SEQS_PER_STEP, r, B_WIDTH), lambda b, pt: (b, 0, 0))

    def page_spec(u, pg):
        return pl.BlockSpec(
            (None, None, B_WIDTH, PAGE_SIZE),
            lambda b, pt: (layer, pt[(b * SEQS_PER_STEP + u) * N_PAGES + pg], 0, 0))

    pages, operands = [], []
    for u in range(SEQS_PER_STEP):
        for cache in (cache_k, cache_v):
            pages += [page_spec(u, pg) for pg in range(N_PAGES)]
            operands += [cache] * N_PAGES
    return pl.pallas_call(
        _moba_sample_kernel,
        grid_spec=pltpu.PrefetchScalarGridSpec(
            num_scalar_prefetch=1,
            grid=(DEC_BATCH // SEQS_PER_STEP,),
            in_specs=[seq_spec(DEC_SEQ * B_HEADS), seq_spec(NEW_PAD), seq_spec(NEW_PAD)] + pages,
            out_specs=seq_spec(DEC_SEQ)),
        out_shape=jax.ShapeDtypeStruct((DEC_BATCH, DEC_SEQ, B_WIDTH), f32),
        compiler_params=_params(("parallel",)),
        name="moba_sample",
    )(page_table, q, k_new, v_new, *operands)


def _even_out_kernel(x_ref, ya_ref, ybt_ref, ybs_ref, w_ref, o_ref):
    i = pl.program_id(0)
    is_prompt = i < N_PROMPT_TILES
    base = x_ref[...] + jnp.dot(ya_ref[...], w_ref[:A_WIDTH, :], preferred_element_type=f32)

    @pl.when(is_prompt)
    def _():
        o_ref[...] = base + lax.dot_general(ybt_ref[...], w_ref[A_WIDTH:, :], _TN,
                                            preferred_element_type=f32)

    @pl.when(jnp.logical_not(is_prompt))
    def _():
        o_ref[...] = base + jnp.dot(ybs_ref[...], w_ref[A_WIDTH:, :], preferred_element_type=f32)


def _even_out(x, ya, yb_t, yb_s, w_out):
    return pl.pallas_call(
        _even_out_kernel,
        grid=(N_TILES,),
        in_specs=[_tile_spec(D_MODEL), _tile_spec(A_WIDTH), _prompt_feature_major_spec(),
                  _const_spec((TM, B_WIDTH)), _const_spec((D_MODEL, D_MODEL))],
        out_specs=_tile_spec(D_MODEL),
        out_shape=jax.ShapeDtypeStruct((N_TOK, D_MODEL), f32),
        compiler_params=_params(("arbitrary",)),
        name="even_out",
    )(x, ya, yb_t, yb_s, w_out)


def _odd_kernel(x_ref, g_ref, wi_ref, cw_ref, wo_ref, st_ref,
                o_ref, cp_ref, cs_ref, gated_ref, ext_ref):
    i = pl.program_id(0)
    is_prompt = i < N_PROMPT_TILES
    h = _rms(x_ref[...], g_ref[...]).astype(bf16)

    @pl.when(i % TILES_PER_SEQ == 0)
    def _():
        ext_ref[CONV_FRONT - SUBLANES:CONV_FRONT, :] = jnp.zeros((SUBLANES, C_WIDTH), f32)

    @pl.when(jnp.logical_not(is_prompt))
    def _():
        ext_ref[:CONV_FRONT, :] = st_ref[...]

    def project(part, c):
        lo = part * C_WIDTH + c * CONV_CHUNK
        return jnp.dot(h, wi_ref[:, lo:lo + CONV_CHUNK], preferred_element_type=f32)

    for c in range(C_WIDTH // CONV_CHUNK):
        cs = slice(c * CONV_CHUNK, (c + 1) * CONV_CHUNK)
        cx = project(2, c) * project(0, c)
        ext_ref[CONV_FRONT:, cs] = cx
        back1 = jnp.where(is_prompt, ext_ref[CONV_FRONT - 1:CONV_FRONT - 1 + TM, cs],
                          ext_ref[CONV_FRONT - DEC_BATCH:CONV_FRONT - DEC_BATCH + TM, cs])
        back2 = jnp.where(is_prompt, ext_ref[CONV_FRONT - 2:CONV_FRONT - 2 + TM, cs],
                          ext_ref[:TM, cs])
        conv = cw_ref[0:1, cs] * back2 + cw_ref[1:2, cs] * back1 + cw_ref[2:3, cs] * cx
        gated_ref[:, cs] = (project(1, c) * conv).astype(bf16)
    o_ref[...] = x_ref[...] + jnp.dot(gated_ref[...], wo_ref[...], preferred_element_type=f32)

    @pl.when(is_prompt)
    def _():
        last = ext_ref[CONV_FRONT + TM - SUBLANES:, :]
        ext_ref[CONV_FRONT - SUBLANES:CONV_FRONT, :] = last
        cp_ref[0] = last

    @pl.when(i == N_TILES - 1)
    def _():
        cs_ref[...] = ext_ref[TM:, :]


def _odd(x, g, w_in, conv_w, w_out, state):
    return pl.pallas_call(
        _odd_kernel,
        grid=(N_TILES,),
        in_specs=[_tile_spec(D_MODEL), _const_spec((1, D_MODEL)),
                  _const_spec((D_MODEL, 3 * C_WIDTH)), _const_spec((CONV_W, C_WIDTH)),
                  _const_spec((C_WIDTH, D_MODEL)), _const_spec((CONV_FRONT, C_WIDTH))],
        out_specs=[_tile_spec(D_MODEL),
                   pl.BlockSpec((1, SUBLANES, C_WIDTH),
                                lambda i: (jnp.minimum(i // TILES_PER_SEQ, BATCH - 1), 0, 0)),
                   pl.BlockSpec((CONV_FRONT, C_WIDTH), lambda i: (0, 0))],
        out_shape=[jax.ShapeDtypeStruct((N_TOK, D_MODEL), f32),
                   jax.ShapeDtypeStruct((BATCH, SUBLANES, C_WIDTH), f32),
                   jax.ShapeDtypeStruct((CONV_FRONT, C_WIDTH), f32)],
        scratch_shapes=[pltpu.VMEM((TM, C_WIDTH), bf16),
                        pltpu.VMEM((CONV_FRONT + TM, C_WIDTH), f32)],
        compiler_params=_params(("arbitrary",)),
        name="odd_mixer",
    )(x, g, w_in, conv_w, w_out, state)


def _step_major(a):
    return jnp.swapaxes(a, 0, 1).reshape(a.shape[0] * a.shape[1], a.shape[2])


def _batch_major(a, steps=DEC_SEQ):
    return jnp.swapaxes(a.reshape(steps, DEC_BATCH, a.shape[-1]), 0, 1)


def kernel(x_prompt, x_sample, cache_k, cache_v, state_conv, page_table, ffn1_norm, ffn1_gate, ffn1_up, ffn1_down, mix_norm, ffn2_norm, ffn2_gate, ffn2_up, ffn2_down, even_w_in, even_w_out, a_v_gain, a_w_s, a_b_s, odd_w_in, odd_conv_w, odd_w_out, final_norm):
    n_phys = cache_k.shape[1]
    pages = lambda c: jnp.transpose(c, (0, 1, 3, 4, 2)).reshape(N_EVEN, n_phys, B_WIDTH, PAGE_SIZE)
    ck, cv = pages(cache_k), pages(cache_v)
    pt = page_table.reshape(-1).astype(jnp.int32)
    row = lambda a: a.reshape(1, -1)
    fg = row(final_norm)
    xs = (x_prompt.reshape(N_PROMPT, D_MODEL), _step_major(x_sample))

    k_p, v_p, k_s, v_s, conv_p, conv_s, av_s = [], [], [], [], [], [], []
    for i in range(DEPTH):
        j = i // 2
        (x,) = _ffn(xs, row(ffn1_norm[i]), ffn1_gate, ffn1_up, ffn1_down, i, fg, first=(i == 0))
        if i % 2 == 0:
            reps = TM // A_CHUNK
            w4 = a_w_s[j][:, :DEC_SEQ, :DEC_SEQ]
            ws_tiled = jnp.stack([
                jnp.tile(a_w_s[j], (1, reps, reps)),
                jnp.repeat(jnp.repeat(w4, DEC_BATCH, axis=1), DEC_BATCH, axis=2)])
            bs_tiled = jnp.stack([
                jnp.tile(a_b_s[j], (1, reps)),
                jnp.repeat(a_b_s[j][:, :DEC_SEQ], DEC_BATCH, axis=1)])[..., None]
            w_in = even_w_in[j].astype(bf16)
            w_row = jnp.concatenate([w_in[:, :2 * A_WIDTH],
                                     w_in[:, 2 * A_WIDTH + B_WIDTH:2 * A_WIDTH + 2 * B_WIDTH]], axis=1)
            w_t = w_in[:, 2 * A_WIDTH:].T
            ya, va, q_t, k_t, v_t, vtb, kb, km = _even_in(
                x, row(mix_norm[i]), w_row, w_t, row(a_v_gain[j]), ws_tiled, bs_tiled)
            kmean = km.reshape(N_TOK // MOBA_BLOCK, B_WIDTH)[:BATCH * N_BLOCKS]
            yb_t = _moba_prompt(q_t, kb, vtb, kmean.reshape(BATCH, N_BLOCKS, B_WIDTH))
            per_seq = lambda a: jnp.transpose(
                a[:, N_PROMPT:].reshape(B_WIDTH, DEC_SEQ, DEC_BATCH), (2, 1, 0))
            padn = lambda a: jnp.pad(per_seq(a), ((0, 0), (0, NEW_PAD - DEC_SEQ), (0, 0)))
            q_rep = jnp.repeat(per_seq(q_t), B_HEADS, axis=1)
            yb_s = _moba_sample(pt, q_rep, padn(k_t), padn(v_t), ck, cv, j)
            x = _even_out(x, ya, yb_t, _step_major(yb_s).astype(bf16), even_w_out[j].astype(bf16))
            heads = lambda a, lead: a.reshape((B_HEADS, B_HEAD_DIM) + lead)
            k_p.append(heads(k_t[:, :N_PROMPT], (BATCH, SEQ)))
            v_p.append(heads(v_t[:, :N_PROMPT], (BATCH, SEQ)))
            k_s.append(heads(k_t[:, N_PROMPT:], (DEC_SEQ, DEC_BATCH)))
            v_s.append(heads(v_t[:, N_PROMPT:], (DEC_SEQ, DEC_BATCH)))
            av_s.append(_batch_major(va))
        else:
            state = jnp.swapaxes(state_conv[j], 0, 1).reshape(CONV_FRONT, C_WIDTH)
            x, cp, cs = _odd(x, row(mix_norm[i]), odd_w_in[j].astype(bf16), odd_conv_w[j],
                             odd_w_out[j].astype(bf16), state)
            conv_p.append(cp[:, SUBLANES - (CONV_W - 1):])
            conv_s.append(_batch_major(cs, CONV_W - 1))
        xs = _ffn((x,), row(ffn2_norm[i]), ffn2_gate, ffn2_up, ffn2_down, i, fg,
                  final=(i == DEPTH - 1))

    y_prompt = xs[0].reshape(BATCH, SEQ, D_MODEL)
    y_sample = _batch_major(xs[1])
    to_rows_p = lambda ts: jnp.transpose(jnp.stack(ts), (0, 3, 4, 1, 2))
    to_rows_s = lambda ts: jnp.transpose(jnp.stack(ts), (0, 4, 3, 1, 2))
    return (y_prompt, y_sample, to_rows_p(k_p), to_rows_p(v_p), jnp.stack(conv_p),
            to_rows_s(k_s), to_rows_s(v_s), jnp.stack(conv_s), jnp.stack(av_s))
```

```python
import functools
import struct

import jax
import jax.numpy as jnp
from jax import lax
from jax.experimental import pallas as pl
from jax.experimental.pallas import tpu as pltpu

D_MODEL = 1024
BATCH = 4
SEQ = 4096
DEPTH = 4
DEC_BATCH = 128
DEC_SEQ = 4
PAST_LEN = 2048
PAGE_SIZE = 128
N_EVEN = (DEPTH + 1) // 2
N_ODD = DEPTH // 2
D_FF = 2816
RMS_EPS = 1e-6
A_WIDTH = D_MODEL // 2
A_CHUNK = 128
A_GROUPS = 4
A_GROUP_DIM = A_WIDTH // A_GROUPS
B_HEAD_DIM = 64
B_WIDTH = D_MODEL - A_WIDTH
B_HEADS = B_WIDTH // B_HEAD_DIM
MOBA_BLOCK = 256
MOBA_TOPK = 3
C_WIDTH = D_MODEL
CONV_W = 3

N_PROMPT = BATCH * SEQ
N_SAMPLE = DEC_BATCH * DEC_SEQ
N_TOK = N_PROMPT + N_SAMPLE
TM = 512
N_TILES = N_TOK // TM
N_PROMPT_TILES = N_PROMPT // TM
TILES_PER_SEQ = SEQ // TM
FF_CHUNK = 256
N_FF_CHUNKS = D_FF // FF_CHUNK
N_PAGES = PAST_LEN // PAGE_SIZE
N_BLOCKS = SEQ // MOBA_BLOCK
N_PAST_BLOCKS = PAST_LEN // MOBA_BLOCK
PAGES_PER_BLOCK = MOBA_BLOCK // PAGE_SIZE
BLOCKS_PER_TILE = TM // MOBA_BLOCK
NEW_PAD = 16
HEAD_PAIR = 2 * B_HEAD_DIM
N_HEAD_PAIRS = B_HEADS // 2
SLOPES = tuple(2.0 ** (-8.0 * (h + 1) / B_HEADS) for h in range(B_HEADS))
NEG = -1e30
SUBLANES = 8
LANES = 128
CONV_FRONT = 2 * DEC_BATCH
CONV_CHUNK = 256
VMEM_LIMIT = 56 * 1024 * 1024

assert N_SAMPLE == TM and N_TOK % TM == 0 and D_FF % FF_CHUNK == 0
assert DEC_BATCH == LANES and CONV_W - 1 == 2 and DEC_SEQ >= CONV_W - 1

_NT = (((1,), (1,)), ((), ()))
_TN = (((0,), (0,)), ((), ()))
bf16 = jnp.bfloat16
f32 = jnp.float32


def _log2(n):
    assert n & (n - 1) == 0
    return n.bit_length() - 1


def _rms(x, g):
    ms = jnp.mean(x * x, axis=-1, keepdims=True)
    return x * lax.rsqrt(ms + RMS_EPS) * g


def _params(sem):
    return pltpu.CompilerParams(dimension_semantics=sem, vmem_limit_bytes=VMEM_LIMIT)


def _const_spec(shape):
    nd = len(shape)
    return pl.BlockSpec(shape, lambda *_: (0,) * nd, pipeline_mode=pl.Buffered(1))


def _tile_spec(width):
    return pl.BlockSpec((TM, width), lambda i: (i, 0))


def _prompt_tile_spec(width):
    return pl.BlockSpec((TM, width), lambda i: (jnp.minimum(i, N_PROMPT_TILES - 1), 0))


def _feature_major_index(i):
    j = jnp.minimum(i, N_PROMPT_TILES - 1)
    return (j // TILES_PER_SEQ, 0, j % TILES_PER_SEQ)


def _feature_major_spec():
    return pl.BlockSpec((1, B_WIDTH, TM), _feature_major_index)


def _layered_feature_major_spec(layer, n_slots):
    assert layer % n_slots == 0
    return pl.BlockSpec((n_slots, 1, B_WIDTH, TM),
                        lambda i: (layer // n_slots,) + _feature_major_index(i))


def _ffn_kernel(*refs, first, final):
    n_x = 2 if first else 1
    x_refs, (g_ref, wg_ref, wu_ref, wd_ref, fg_ref) = refs[:n_x], refs[n_x:n_x + 5]
    o_refs, acc_ref = refs[n_x + 5:-1], refs[-1]
    i = pl.program_id(0)
    is_prompt = i < N_PROMPT_TILES
    if first:
        x = jnp.where(is_prompt, x_refs[0][...], x_refs[1][...])
    else:
        x = x_refs[0][...]
    h = _rms(x, g_ref[...]).astype(bf16)
    for c in range(N_FF_CHUNKS):
        sl = slice(c * FF_CHUNK, (c + 1) * FF_CHUNK)
        gate = jnp.dot(h, wg_ref[:, sl].astype(bf16), preferred_element_type=f32)
        up = jnp.dot(h, wu_ref[:, sl].astype(bf16), preferred_element_type=f32)
        act = (gate / (1.0 + jnp.exp(-gate)) * up).astype(bf16)
        down = jnp.dot(act, wd_ref[sl, :].astype(bf16), preferred_element_type=f32)
        if c == 0:
            acc_ref[...] = down
        else:
            acc_ref[...] += down
    y = x + 0.5 * acc_ref[...]
    if final:
        y = _rms(y, fg_ref[...])

        @pl.when(is_prompt)
        def _():
            o_refs[0][...] = y

        @pl.when(jnp.logical_not(is_prompt))
        def _():
            o_refs[1][...] = y
    else:
        o_refs[0][...] = y


def _ffn(xs, g, wg, wu, wd, layer, fg, first=False, final=False):
    def layer_spec(rows, cols):
        return pl.BlockSpec((None, rows, cols), lambda i: (layer, 0, 0),
                            pipeline_mode=pl.Buffered(1))

    sample_spec = pl.BlockSpec((TM, D_MODEL), lambda i: (0, 0))
    x_specs = [_prompt_tile_spec(D_MODEL), sample_spec] if first else [_tile_spec(D_MODEL)]
    if final:
        out_specs = [_prompt_tile_spec(D_MODEL), sample_spec]
        out_shape = [jax.ShapeDtypeStruct((N_PROMPT, D_MODEL), f32),
                     jax.ShapeDtypeStruct((N_SAMPLE, D_MODEL), f32)]
    else:
        out_specs = [_tile_spec(D_MODEL)]
        out_shape = [jax.ShapeDtypeStruct((N_TOK, D_MODEL), f32)]
    return pl.pallas_call(
        functools.partial(_ffn_kernel, first=first, final=final),
        grid=(N_TILES,),
        in_specs=x_specs + [_const_spec((1, D_MODEL)),
                            layer_spec(D_MODEL, D_FF), layer_spec(D_MODEL, D_FF),
                            layer_spec(D_FF, D_MODEL), _const_spec((1, D_MODEL))],
        out_specs=out_specs,
        out_shape=out_shape,
        scratch_shapes=[pltpu.VMEM((TM, D_MODEL), f32)],
        compiler_params=_params(("arbitrary",)),
        name="ffn",
    )(*xs, g, wg, wu, wd, fg)


ROW_COLS = 2 * A_WIDTH + B_WIDTH
T_ROWS = 3 * B_WIDTH


def _even_in_kernel(*refs, n_prev):
    (x_ref, g_ref, w_ref, wt_ref, gain_ref, ws_ref, bs_ref) = refs[:7]
    (ya_ref, vs_ref, qt_ref, kt_ref, vt_ref, vtb_ref, kb_ref, km_ref, st_ref,
     z_ref, zt_ref) = refs[7 + n_prev:]
    i = pl.program_id(0)
    is_prompt = i < N_PROMPT_TILES
    h = _rms(x_ref[...], g_ref[...]).astype(bf16)
    z_ref[...] = jnp.dot(h, w_ref[...], preferred_element_type=f32)
    zt_ref[...] = lax.dot_general(wt_ref[...], h, _NT, preferred_element_type=f32)

    def gelu(a):
        return a * (0.5 * (1.0 + jnp.tanh(0.7978845608028654 * (a + 0.044715 * (a * a * a)))))

    row = lax.broadcasted_iota(jnp.int32, (TM, TM), 0)
    col = lax.broadcasted_iota(jnp.int32, (TM, TM), 1)
    chunk = lambda r: jnp.where(is_prompt, r >> _log2(A_CHUNK), r & (DEC_BATCH - 1))
    keep = (chunk(row) == chunk(col)) & (col <= row)
    for g in range(A_GROUPS):
        gs = slice(g * A_GROUP_DIM, (g + 1) * A_GROUP_DIM)
        u = gelu(z_ref[:, gs])
        v = gelu(z_ref[:, A_WIDTH + g * A_GROUP_DIM:A_WIDTH + (g + 1) * A_GROUP_DIM])
        vn = _rms(v, gain_ref[:, gs])
        w = jnp.where(keep, ws_ref[0, g], 0.0).astype(bf16)
        mixed = jnp.dot(w, vn.astype(bf16), preferred_element_type=f32) + bs_ref[0, g]
        ya_ref[:, gs] = (u * mixed).astype(bf16)

        @pl.when(i == N_TILES - 1)
        def _():
            vs_ref[:, gs] = vn

    k = z_ref[:, 2 * A_WIDTH:]
    kb_ref[...] = k.astype(bf16)
    for blk in range(BLOCKS_PER_TILE):
        ks = jnp.sum(k[blk * MOBA_BLOCK:(blk + 1) * MOBA_BLOCK], axis=0, keepdims=True)
        km_ref[0, blk:blk + 1, :] = ks * (1.0 / MOBA_BLOCK)

    @pl.when(is_prompt)
    def _():
        qt_ref[0] = zt_ref[:B_WIDTH, :]
        kt = zt_ref[B_WIDTH:2 * B_WIDTH, :]
        vt = zt_ref[2 * B_WIDTH:, :]
        for slot in range(kt_ref.shape[0]):
            kt_ref[slot, 0] = kt
            vt_ref[slot, 0] = vt
        vtb_ref[0] = vt.astype(bf16)

    @pl.when(i == N_TILES - 1)
    def _():
        for part in range(3):
            for t in range(DEC_SEQ):
                st_ref[part, t] = zt_ref[part * B_WIDTH:(part + 1) * B_WIDTH,
                                         t * DEC_BATCH:(t + 1) * DEC_BATCH]


def _even_in(x, g, w_row, w_t, gain, ws_tiled, bs_tiled, layer, kv_stacks):
    kind = lambda i: (i // N_PROMPT_TILES, 0, 0, 0)
    fm = lambda dt: jax.ShapeDtypeStruct((BATCH, B_WIDTH, SEQ), dt)
    stack = jax.ShapeDtypeStruct((N_EVEN, BATCH, B_WIDTH, SEQ), f32)
    n_in = 7
    stack_spec = _layered_feature_major_spec(layer, 1 if kv_stacks else N_EVEN)
    return pl.pallas_call(
        functools.partial(_even_in_kernel, n_prev=len(kv_stacks)),
        grid=(N_TILES,),
        in_specs=[_tile_spec(D_MODEL), _const_spec((1, D_MODEL)),
                  _const_spec((D_MODEL, ROW_COLS)), _const_spec((T_ROWS, D_MODEL)),
                  _const_spec((1, A_WIDTH)),
                  pl.BlockSpec((1, A_GROUPS, TM, TM), kind),
                  pl.BlockSpec((1, A_GROUPS, TM, 1), kind)]
        + [pl.BlockSpec(memory_space=pl.ANY)] * len(kv_stacks),
        out_specs=[_tile_spec(A_WIDTH),
                   pl.BlockSpec((TM, A_WIDTH), lambda i: (0, 0)),
                   _feature_major_spec(), stack_spec, stack_spec,
                   _feature_major_spec(),
                   _tile_spec(B_WIDTH),
                   pl.BlockSpec((1, BLOCKS_PER_TILE, B_WIDTH), lambda i: (i, 0, 0)),
                   pl.BlockSpec((3, DEC_SEQ, B_WIDTH, DEC_BATCH), lambda i: (0, 0, 0, 0))],
        out_shape=[jax.ShapeDtypeStruct((N_TOK, A_WIDTH), bf16),
                   jax.ShapeDtypeStruct((N_SAMPLE, A_WIDTH), f32),
                   fm(f32), stack, stack, fm(bf16),
                   jax.ShapeDtypeStruct((N_TOK, B_WIDTH), bf16),
                   jax.ShapeDtypeStruct((N_TILES, BLOCKS_PER_TILE, B_WIDTH), f32),
                   jax.ShapeDtypeStruct((3, DEC_SEQ, B_WIDTH, DEC_BATCH), f32)],
        scratch_shapes=[pltpu.VMEM((TM, ROW_COLS), f32), pltpu.VMEM((T_ROWS, TM), f32)],
        input_output_aliases={n_in + s: 3 + s for s in range(len(kv_stacks))},
        compiler_params=_params(("arbitrary",)),
        name="even_in",
    )(x, g, w_row, w_t, gain, ws_tiled, bs_tiled, *kv_stacks)


def _select_blocks(gate, n_valid, n_past, axis):
    blk = lax.broadcasted_iota(jnp.int32, gate.shape, axis)
    valid = blk < n_valid
    g = jnp.where(valid, gate, -jnp.inf)
    sel = jnp.zeros(gate.shape, jnp.bool_)
    for _ in range(min(MOBA_TOPK, n_past)):
        mx = jnp.max(g, axis=axis, keepdims=True)
        idx = jnp.min(jnp.where(g == mx, blk, n_past), axis=axis, keepdims=True)
        hit = blk == idx
        sel = sel | (hit & valid)
        g = jnp.where(hit, -jnp.inf, g)
    return sel


PAIR_COLS = 2 * MOBA_BLOCK
ALL_COLS = B_HEADS * MOBA_BLOCK


LOG2E = 1.4426950408889634


def _round_bf16(x):
    bits = struct.unpack("<I", struct.pack("<f", x))[0]
    bits = (bits + 0x7FFF + ((bits >> 16) & 1)) & 0xFFFF0000
    return struct.unpack("<f", struct.pack("<I", bits))[0]


def _bf16_terms(x, n):
    terms = []
    for _ in range(n):
        t = _round_bf16(x)
        terms.append(t)
        x -= t
    return terms


SLOPE_TERMS = 4
SUM_ROWS = 16


def _moba_prompt_kernel(q_ref, k_ref, v_ref, km_ref, o_ref,
                        kpos_ref, qbd_ref, bias_ref, s0_ref, s1_ref, m_ref, acc_ref):
    b = pl.program_id(0)
    i = pl.program_id(1)

    @pl.when((b == 0) & (i == 0))
    def _():
        key = lax.broadcasted_iota(jnp.int32, (MOBA_BLOCK, HEAD_PAIR), 0)
        col = lax.broadcasted_iota(jnp.int32, (MOBA_BLOCK, HEAD_PAIR), 1)
        kpos_ref[...] = jnp.where(col < SLOPE_TERMS, key, 0).astype(bf16)
        r = lax.broadcasted_iota(jnp.int32, (HEAD_PAIR, PAIR_COLS), 0)
        e_of_col = lax.broadcasted_iota(jnp.int32, (HEAD_PAIR, PAIR_COLS), 1) >> _log2(MOBA_BLOCK)
        for p in range(N_HEAD_PAIRS):
            slope_rows = jnp.zeros((HEAD_PAIR, PAIR_COLS), f32)
            for e in range(2):
                for t, term in enumerate(_bf16_terms(SLOPES[2 * p + e] * LOG2E, SLOPE_TERMS)):
                    slope_rows = jnp.where((r == t) & (e_of_col == e), term, slope_rows)
            qbd_ref[p, HEAD_PAIR:, :] = slope_rows.astype(bf16)

    feat = lax.broadcasted_iota(jnp.int32, (HEAD_PAIR, PAIR_COLS), 0)
    qcol = lax.broadcasted_iota(jnp.int32, (HEAD_PAIR, PAIR_COLS), 1)
    mine = (feat >> _log2(B_HEAD_DIM)) == (qcol >> _log2(MOBA_BLOCK))
    blk = lax.broadcasted_iota(jnp.int32, (N_BLOCKS, PAIR_COLS), 0)
    bcol = lax.broadcasted_iota(jnp.int32, (N_BLOCKS, PAIR_COLS), 1)
    blocks_back = (i - blk).astype(f32) * float(MOBA_BLOCK)
    for p in range(N_HEAD_PAIRS):
        ls = slice(p * HEAD_PAIR, (p + 1) * HEAD_PAIR)
        q2 = q_ref[0, ls, :]
        qmf = jnp.where(mine, jnp.concatenate([q2, q2], axis=1), 0.0)
        qbd_ref[p, :HEAD_PAIR, :] = (qmf * (LOG2E * B_HEAD_DIM ** -0.5)).astype(bf16)
        gate = jnp.dot(km_ref[0, :, ls], qmf, precision=lax.Precision.HIGHEST,
                       preferred_element_type=f32)
        sel = _select_blocks(gate, i, N_BLOCKS - 1, 0)
        slope = jnp.where(bcol < MOBA_BLOCK, SLOPES[2 * p] * LOG2E, SLOPES[2 * p + 1] * LOG2E)
        bias = jnp.where(sel, -slope * blocks_back, NEG)
        bias_ref[:, p * PAIR_COLS:(p + 1) * PAIR_COLS] = jnp.where(blk == i, 0.0, bias)
    m_ref[...] = jnp.full((1, ALL_COLS), NEG, f32)
    acc_ref[...] = jnp.zeros((B_HEADS, B_HEAD_DIM + SUM_ROWS, MOBA_BLOCK), f32)

    def scores(n, s_ref, own):
        start = pl.multiple_of(n * MOBA_BLOCK, MOBA_BLOCK)
        for p in range(N_HEAD_PAIRS):
            kn = k_ref[pl.ds(start, MOBA_BLOCK), p * HEAD_PAIR:(p + 1) * HEAD_PAIR]
            kn = jnp.concatenate([kn, kpos_ref[...]], axis=1)
            s = jnp.dot(kn, qbd_ref[p], preferred_element_type=f32)
            if own:
                key = lax.broadcasted_iota(jnp.int32, (MOBA_BLOCK, PAIR_COLS), 0)
                query = lax.broadcasted_iota(jnp.int32, (MOBA_BLOCK, PAIR_COLS), 1) & (MOBA_BLOCK - 1)
                s = jnp.where(key <= query, s, NEG)
            s_ref[p] = s

    def absorb(n, s_ref):
        start = pl.multiple_of(n * MOBA_BLOCK, MOBA_BLOCK)
        for p in range(N_HEAD_PAIRS):
            cs = slice(p * PAIR_COLS, (p + 1) * PAIR_COLS)
            s = s_ref[p]
            brow = bias_ref[pl.ds(n, 1), cs]
            m_old = m_ref[:, cs]
            m_new = jnp.maximum(m_old, jnp.max(s, axis=0, keepdims=True) + brow)
            alpha = jnp.exp2(m_old - m_new)
            peb = jnp.exp2(s - (m_new - brow)).astype(bf16)
            m_ref[:, cs] = m_new
            for e in range(2):
                h = 2 * p + e
                es = slice(e * MOBA_BLOCK, (e + 1) * MOBA_BLOCK)
                vn = v_ref[0, h * B_HEAD_DIM:(h + 1) * B_HEAD_DIM, pl.ds(start, MOBA_BLOCK)]
                vn = jnp.concatenate([vn, jnp.ones((SUM_ROWS, MOBA_BLOCK), bf16)], axis=0)
                acc_ref[h] = alpha[:, es] * acc_ref[h] + jnp.dot(
                    vn, peb[:, es], preferred_element_type=f32)

    scores(i, s0_ref, True)

    def two_steps(jj, carry):
        j = 2 * jj
        scores(j, s1_ref, False)
        absorb(jnp.where(jj == 0, i, j - 1), s0_ref)
        scores(j + 1, s0_ref, False)
        absorb(j, s1_ref)
        return carry

    lax.fori_loop(0, i // 2, two_steps, 0)

    @pl.when(i % 2 == 0)
    def _():
        absorb(jnp.maximum(i - 1, 0), s0_ref)

    @pl.when(i % 2 == 1)
    def _():
        scores(i - 1, s1_ref, False)
        absorb(jnp.where(i == 1, i, i - 2), s0_ref)
        absorb(i - 1, s1_ref)

    for h in range(B_HEADS):
        o_ref[0, h * B_HEAD_DIM:(h + 1) * B_HEAD_DIM, :] = (
            acc_ref[h, :B_HEAD_DIM, :] / acc_ref[h, B_HEAD_DIM:B_HEAD_DIM + 1, :]).astype(bf16)


def _moba_prompt(q_t, kb, vtb, kmean):
    qspec = pl.BlockSpec((1, B_WIDTH, MOBA_BLOCK), lambda b, i: (b, 0, i))
    return pl.pallas_call(
        _moba_prompt_kernel,
        grid=(BATCH, N_BLOCKS),
        in_specs=[qspec,
                  pl.BlockSpec((SEQ, B_WIDTH), lambda b, i: (b, 0)),
                  pl.BlockSpec((1, B_WIDTH, SEQ), lambda b, i: (b, 0, 0)),
                  pl.BlockSpec((1, N_BLOCKS, B_WIDTH), lambda b, i: (b, 0, 0))],
        out_specs=qspec,
        out_shape=jax.ShapeDtypeStruct((BATCH, B_WIDTH, SEQ), bf16),
        scratch_shapes=[pltpu.VMEM((MOBA_BLOCK, HEAD_PAIR), bf16),
                        pltpu.VMEM((N_HEAD_PAIRS, 2 * HEAD_PAIR, PAIR_COLS), bf16),
                        pltpu.VMEM((N_BLOCKS, ALL_COLS), f32),
                        pltpu.VMEM((N_HEAD_PAIRS, MOBA_BLOCK, PAIR_COLS), f32),
                        pltpu.VMEM((N_HEAD_PAIRS, MOBA_BLOCK, PAIR_COLS), f32),
                        pltpu.VMEM((1, ALL_COLS), f32),
                        pltpu.VMEM((B_HEADS, B_HEAD_DIM + SUM_ROWS, MOBA_BLOCK), f32)],
        compiler_params=_params(("arbitrary", "arbitrary")),
        name="moba_prompt",
    )(q_t, kb, vtb, kmean)


SEQS_PER_STEP = 2


def _moba_sample_kernel(pt_ref, q_ref, kn_ref, vn_ref, *refs):
    seqs = range(SEQS_PER_STEP)
    k_refs = [refs[2 * u * N_PAGES:(2 * u + 1) * N_PAGES] for u in seqs]
    v_refs = [refs[(2 * u + 1) * N_PAGES:(2 * u + 2) * N_PAGES] for u in seqs]
    o_ref = refs[2 * SEQS_PER_STEP * N_PAGES]
    rows = DEC_SEQ * B_HEADS
    row = lax.broadcasted_iota(jnp.int32, (rows, B_WIDTH), 0)
    lane = lax.broadcasted_iota(jnp.int32, (rows, B_WIDTH), 1)
    head_lanes = (lane >> _log2(B_HEAD_DIM)) == (row & (B_HEADS - 1))
    r1 = lax.broadcasted_iota(jnp.int32, (rows, 1), 0)
    hrow = r1 & (B_HEADS - 1)
    slope = jnp.zeros((rows, 1), f32)
    for h in range(B_HEADS):
        slope = jnp.where(hrow == h, SLOPES[h], slope)
    tq = r1 >> _log2(B_HEADS)
    kcol = lax.broadcasted_iota(jnp.int32, (rows, PAGE_SIZE), 1)
    tk = lax.broadcasted_iota(jnp.int32, (rows, NEW_PAD), 1)
    flane = lax.broadcasted_iota(jnp.int32, (B_WIDTH, PAGE_SIZE), 1)

    qbd, raw, raw_new = [], [], []
    for u in seqs:
        qf = jnp.where(head_lanes, q_ref[u], 0.0)
        qb = (qf * (B_HEAD_DIM ** -0.5)).astype(bf16)
        qbd.append(qf)
        raw.append([jnp.dot(qb, k_refs[u][pg][...].astype(bf16), preferred_element_type=f32)
                    for pg in range(N_PAGES)])
        raw_new.append(lax.dot_general(qb, kn_ref[u].astype(bf16), _NT,
                                       preferred_element_type=f32))

    kmean = []
    for u in seqs:
        km = jnp.zeros((B_WIDTH, PAGE_SIZE), f32)
        for blk in range(N_PAST_BLOCKS):
            tot = k_refs[u][blk * PAGES_PER_BLOCK][...]
            for pg in range(blk * PAGES_PER_BLOCK + 1, (blk + 1) * PAGES_PER_BLOCK):
                tot = tot + k_refs[u][pg][...]
            mean = jnp.sum(tot, axis=1, keepdims=True) * (1.0 / MOBA_BLOCK)
            km = jnp.where(flane == blk, mean, km)
        kmean.append(km)
    gate = [jnp.dot(qbd[u], kmean[u], precision=lax.Precision.HIGHEST,
                    preferred_element_type=f32) for u in seqs]

    scores, s_new, m = [], [], []
    for u in seqs:
        sel = _select_blocks(gate[u], N_PAST_BLOCKS, N_PAST_BLOCKS, 1)
        blk_bias = jnp.where(sel, 0.0, NEG)
        sn = jnp.where(tk <= tq, raw_new[u] - slope * (tq - tk).astype(f32), NEG)
        mu = jnp.max(sn, axis=1, keepdims=True)
        su = []
        for pg in range(N_PAGES):
            dist = (tq + (PAST_LEN - pg * PAGE_SIZE) - kcol).astype(f32)
            blk = pg // PAGES_PER_BLOCK
            s = raw[u][pg] - slope * dist + blk_bias[:, blk:blk + 1]
            mu = jnp.maximum(mu, jnp.max(s, axis=1, keepdims=True))
            su.append(s)
        scores.append(su)
        s_new.append(sn)
        m.append(mu)

    for u in seqs:
        p_new = jnp.exp(s_new[u] - m[u])
        l = jnp.sum(p_new, axis=1, keepdims=True)
        acc = jnp.dot(p_new.astype(bf16), vn_ref[u].astype(bf16), preferred_element_type=f32)
        for pg in range(N_PAGES):
            pe = jnp.exp(scores[u][pg] - m[u])
            l = l + jnp.sum(pe, axis=1, keepdims=True)
            acc = acc + lax.dot_general(pe.astype(bf16), v_refs[u][pg][...].astype(bf16), _NT,
                                        preferred_element_type=f32)
        out = jnp.where(head_lanes, acc / l, 0.0)
        o_ref[u] = jnp.sum(out.reshape(DEC_SEQ, B_HEADS, B_WIDTH), axis=1)


def _moba_sample(page_table, q, k_new, v_new, cache_k, cache_v, layer):
    seq_spec = lambda r: pl.BlockSpec((SEQS_PER_STEP, r, B_WIDTH), lambda b, pt: (b, 0, 0))

    def page_spec(u, pg):
        return pl.BlockSpec(
            (None, None, B_WIDTH, PAGE_SIZE),
            lambda b, pt: (layer, pt[(b * SEQS_PER_STEP + u) * N_PAGES + pg], 0, 0))

    pages, operands = [], []
    for u in range(SEQS_PER_STEP):
        for cache in (cache_k, cache_v):
            pages += [page_spec(u, pg) for pg in range(N_PAGES)]
            operands += [cache] * N_PAGES
    return pl.pallas_call(
        _moba_sample_kernel,
        grid_spec=pltpu.PrefetchScalarGridSpec(
            num_scalar_prefetch=1,
            grid=(DEC_BATCH // SEQS_PER_STEP,),
            in_specs=[seq_spec(DEC_SEQ * B_HEADS), seq_spec(NEW_PAD), seq_spec(NEW_PAD)] + pages,
            out_specs=seq_spec(DEC_SEQ)),
        out_shape=jax.ShapeDtypeStruct((DEC_BATCH, DEC_SEQ, B_WIDTH), f32),
        compiler_params=_params(("parallel",)),
        name="moba_sample",
    )(page_table, q, k_new, v_new, *operands)


def _even_out_kernel(x_ref, ya_ref, ybt_ref, ybs_ref, w_ref, o_ref):
    i = pl.program_id(0)
    is_prompt = i < N_PROMPT_TILES
    base = x_ref[...] + jnp.dot(ya_ref[...], w_ref[:A_WIDTH, :], preferred_element_type=f32)

    @pl.when(is_prompt)
    def _():
        o_ref[...] = base + lax.dot_general(ybt_ref[0], w_ref[A_WIDTH:, :], _TN,
                                            preferred_element_type=f32)

    @pl.when(jnp.logical_not(is_prompt))
    def _():
        o_ref[...] = base + jnp.dot(ybs_ref[...], w_ref[A_WIDTH:, :], preferred_element_type=f32)


def _even_out(x, ya, yb_t, yb_s, w_out):
    return pl.pallas_call(
        _even_out_kernel,
        grid=(N_TILES,),
        in_specs=[_tile_spec(D_MODEL), _tile_spec(A_WIDTH), _feature_major_spec(),
                  _const_spec((TM, B_WIDTH)), _const_spec((D_MODEL, D_MODEL))],
        out_specs=_tile_spec(D_MODEL),
        out_shape=jax.ShapeDtypeStruct((N_TOK, D_MODEL), f32),
        compiler_params=_params(("arbitrary",)),
        name="even_out",
    )(x, ya, yb_t, yb_s, w_out)


def _odd_kernel(x_ref, g_ref, wi_ref, cw_ref, wo_ref, st_ref,
                o_ref, cp_ref, cs_ref, gated_ref, ext_ref):
    i = pl.program_id(0)
    is_prompt = i < N_PROMPT_TILES
    h = _rms(x_ref[...], g_ref[...]).astype(bf16)

    @pl.when(i % TILES_PER_SEQ == 0)
    def _():
        ext_ref[CONV_FRONT - SUBLANES:CONV_FRONT, :] = jnp.zeros((SUBLANES, C_WIDTH), f32)

    @pl.when(jnp.logical_not(is_prompt))
    def _():
        ext_ref[:CONV_FRONT, :] = st_ref[...]

    def project(part, c):
        lo = part * C_WIDTH + c * CONV_CHUNK
        return jnp.dot(h, wi_ref[:, lo:lo + CONV_CHUNK], preferred_element_type=f32)

    for c in range(C_WIDTH // CONV_CHUNK):
        cs = slice(c * CONV_CHUNK, (c + 1) * CONV_CHUNK)
        cx = project(2, c) * project(0, c)
        ext_ref[CONV_FRONT:, cs] = cx
        back1 = jnp.where(is_prompt, ext_ref[CONV_FRONT - 1:CONV_FRONT - 1 + TM, cs],
                          ext_ref[CONV_FRONT - DEC_BATCH:CONV_FRONT - DEC_BATCH + TM, cs])
        back2 = jnp.where(is_prompt, ext_ref[CONV_FRONT - 2:CONV_FRONT - 2 + TM, cs],
                          ext_ref[:TM, cs])
        conv = cw_ref[0:1, cs] * back2 + cw_ref[1:2, cs] * back1 + cw_ref[2:3, cs] * cx
        gated_ref[:, cs] = (project(1, c) * conv).astype(bf16)
    o_ref[...] = x_ref[...] + jnp.dot(gated_ref[...], wo_ref[...], preferred_element_type=f32)

    @pl.when(is_prompt)
    def _():
        last = ext_ref[CONV_FRONT + TM - SUBLANES:, :]
        ext_ref[CONV_FRONT - SUBLANES:CONV_FRONT, :] = last
        cp_ref[0] = last

    @pl.when(i == N_TILES - 1)
    def _():
        cs_ref[...] = ext_ref[TM:, :]


def _odd(x, g, w_in, conv_w, w_out, state):
    return pl.pallas_call(
        _odd_kernel,
        grid=(N_TILES,),
        in_specs=[_tile_spec(D_MODEL), _const_spec((1, D_MODEL)),
                  _const_spec((D_MODEL, 3 * C_WIDTH)), _const_spec((CONV_W, C_WIDTH)),
                  _const_spec((C_WIDTH, D_MODEL)), _const_spec((CONV_FRONT, C_WIDTH))],
        out_specs=[_tile_spec(D_MODEL),
                   pl.BlockSpec((1, SUBLANES, C_WIDTH),
                                lambda i: (jnp.minimum(i // TILES_PER_SEQ, BATCH - 1), 0, 0)),
                   pl.BlockSpec((CONV_FRONT, C_WIDTH), lambda i: (0, 0))],
        out_shape=[jax.ShapeDtypeStruct((N_TOK, D_MODEL), f32),
                   jax.ShapeDtypeStruct((BATCH, SUBLANES, C_WIDTH), f32),
                   jax.ShapeDtypeStruct((CONV_FRONT, C_WIDTH), f32)],
        scratch_shapes=[pltpu.VMEM((TM, C_WIDTH), bf16),
                        pltpu.VMEM((CONV_FRONT + TM, C_WIDTH), f32)],
        compiler_params=_params(("arbitrary",)),
        name="odd_mixer",
    )(x, g, w_in, conv_w, w_out, state)


def _step_major(a):
    return jnp.swapaxes(a, 0, 1).reshape(a.shape[0] * a.shape[1], a.shape[2])


def _batch_major(a, steps=DEC_SEQ):
    return jnp.swapaxes(a.reshape(steps, DEC_BATCH, a.shape[-1]), 0, 1)


def kernel(x_prompt, x_sample, cache_k, cache_v, state_conv, page_table, ffn1_norm, ffn1_gate, ffn1_up, ffn1_down, mix_norm, ffn2_norm, ffn2_gate, ffn2_up, ffn2_down, even_w_in, even_w_out, a_v_gain, a_w_s, a_b_s, odd_w_in, odd_conv_w, odd_w_out, final_norm):
    n_phys = cache_k.shape[1]
    pages = lambda c: jnp.transpose(c, (0, 1, 3, 4, 2)).reshape(N_EVEN, n_phys, B_WIDTH, PAGE_SIZE)
    ck, cv = pages(cache_k), pages(cache_v)
    pt = page_table.reshape(-1).astype(jnp.int32)
    row = lambda a: a.reshape(1, -1)
    fg = row(final_norm)
    xs = (x_prompt.reshape(N_PROMPT, D_MODEL), _step_major(x_sample))

    k_s, v_s, conv_p, conv_s, av_s = [], [], [], [], []
    kv_stacks = ()
    for i in range(DEPTH):
        j = i // 2
        (x,) = _ffn(xs, row(ffn1_norm[i]), ffn1_gate, ffn1_up, ffn1_down, i, fg, first=(i == 0))
        if i % 2 == 0:
            reps = TM // A_CHUNK
            w4 = a_w_s[j][:, :DEC_SEQ, :DEC_SEQ]
            ws_tiled = jnp.stack([
                jnp.tile(a_w_s[j], (1, reps, reps)),
                jnp.repeat(jnp.repeat(w4, DEC_BATCH, axis=1), DEC_BATCH, axis=2)])
            bs_tiled = jnp.stack([
                jnp.tile(a_b_s[j], (1, reps)),
                jnp.repeat(a_b_s[j][:, :DEC_SEQ], DEC_BATCH, axis=1)])[..., None]
            w_in = even_w_in[j].astype(bf16)
            w_row = jnp.concatenate([w_in[:, :2 * A_WIDTH],
                                     w_in[:, 2 * A_WIDTH + B_WIDTH:2 * A_WIDTH + 2 * B_WIDTH]], axis=1)
            w_t = w_in[:, 2 * A_WIDTH:].T
            ya, va, q_t, k_stack, v_stack, vtb, kb, km, st = _even_in(
                x, row(mix_norm[i]), w_row, w_t, row(a_v_gain[j]), ws_tiled, bs_tiled, j, kv_stacks)
            kv_stacks = (k_stack, v_stack)
            kmean = km.reshape(N_TOK // MOBA_BLOCK, B_WIDTH)[:BATCH * N_BLOCKS]
            yb_t = _moba_prompt(q_t, kb, vtb, kmean.reshape(BATCH, N_BLOCKS, B_WIDTH))
            per_seq = jnp.transpose(st, (0, 3, 1, 2))
            padn = lambda a: jnp.pad(a, ((0, 0), (0, NEW_PAD - DEC_SEQ), (0, 0)))
            q_rep = jnp.repeat(per_seq[0], B_HEADS, axis=1)
            yb_s = _moba_sample(pt, q_rep, padn(per_seq[1]), padn(per_seq[2]), ck, cv, j)
            x = _even_out(x, ya, yb_t, _step_major(yb_s).astype(bf16), even_w_out[j].astype(bf16))
            k_s.append(st[1].reshape(DEC_SEQ, B_HEADS, B_HEAD_DIM, DEC_BATCH))
            v_s.append(st[2].reshape(DEC_SEQ, B_HEADS, B_HEAD_DIM, DEC_BATCH))
            av_s.append(_batch_major(va))
        else:
            state = jnp.swapaxes(state_conv[j], 0, 1).reshape(CONV_FRONT, C_WIDTH)
            x, cp, cs = _odd(x, row(mix_norm[i]), odd_w_in[j].astype(bf16), odd_conv_w[j],
                             odd_w_out[j].astype(bf16), state)
            conv_p.append(cp[:, SUBLANES - (CONV_W - 1):])
            conv_s.append(_batch_major(cs, CONV_W - 1))
        xs = _ffn((x,), row(ffn2_norm[i]), ffn2_gate, ffn2_up, ffn2_down, i, fg,
                  final=(i == DEPTH - 1))

    y_prompt = xs[0].reshape(BATCH, SEQ, D_MODEL)
    y_sample = _batch_major(xs[1])
    to_rows_p = lambda a: jnp.transpose(
        a.reshape(N_EVEN, BATCH, B_HEADS, B_HEAD_DIM, SEQ), (0, 1, 4, 2, 3))
    to_rows_s = lambda ts: jnp.transpose(jnp.stack(ts), (0, 4, 1, 2, 3))
    return (y_prompt, y_sample, to_rows_p(kv_stacks[0]), to_rows_p(kv_stacks[1]),
            jnp.stack(conv_p),
            to_rows_s(k_s), to_rows_s(v_s), jnp.stack(conv_s), jnp.stack(av_s))
```

```python
import functools
import struct

import jax
import jax.numpy as jnp
from jax import lax
from jax.experimental import pallas as pl
from jax.experimental.pallas import tpu as pltpu

D_MODEL = 1024
BATCH = 4
SEQ = 4096
DEPTH = 4
DEC_BATCH = 128
DEC_SEQ = 4
PAST_LEN = 2048
PAGE_SIZE = 128
N_EVEN = (DEPTH + 1) // 2
N_ODD = DEPTH // 2
D_FF = 2816
RMS_EPS = 1e-6
A_WIDTH = D_MODEL // 2
A_CHUNK = 128
A_GROUPS = 4
A_GROUP_DIM = A_WIDTH // A_GROUPS
B_HEAD_DIM = 64
B_WIDTH = D_MODEL - A_WIDTH
B_HEADS = B_WIDTH // B_HEAD_DIM
MOBA_BLOCK = 256
MOBA_TOPK = 3
C_WIDTH = D_MODEL
CONV_W = 3

N_PROMPT = BATCH * SEQ
N_SAMPLE = DEC_BATCH * DEC_SEQ
N_TOK = N_PROMPT + N_SAMPLE
TM = 512
N_TILES = N_TOK // TM
N_PROMPT_TILES = N_PROMPT // TM
TILES_PER_SEQ = SEQ // TM
FF_CHUNK = 256
N_FF_CHUNKS = D_FF // FF_CHUNK
N_PAGES = PAST_LEN // PAGE_SIZE
N_BLOCKS = SEQ // MOBA_BLOCK
N_PAST_BLOCKS = PAST_LEN // MOBA_BLOCK
PAGES_PER_BLOCK = MOBA_BLOCK // PAGE_SIZE
BLOCKS_PER_TILE = TM // MOBA_BLOCK
NEW_PAD = 16
HEAD_PAIR = 2 * B_HEAD_DIM
N_HEAD_PAIRS = B_HEADS // 2
SLOPES = tuple(2.0 ** (-8.0 * (h + 1) / B_HEADS) for h in range(B_HEADS))
NEG = -1e30
SUBLANES = 8
LANES = 128
CONV_FRONT = 2 * DEC_BATCH
CONV_CHUNK = 256
VMEM_LIMIT = 56 * 1024 * 1024

assert N_SAMPLE == TM and N_TOK % TM == 0 and D_FF % FF_CHUNK == 0
assert DEC_BATCH == LANES and CONV_W - 1 == 2 and DEC_SEQ >= CONV_W - 1

_NT = (((1,), (1,)), ((), ()))
_TN = (((0,), (0,)), ((), ()))
bf16 = jnp.bfloat16
f32 = jnp.float32


def _log2(n):
    assert n & (n - 1) == 0
    return n.bit_length() - 1


def _rms(x, g):
    ms = jnp.mean(x * x, axis=-1, keepdims=True)
    return x * lax.rsqrt(ms + RMS_EPS) * g


def _params(sem):
    return pltpu.CompilerParams(dimension_semantics=sem, vmem_limit_bytes=VMEM_LIMIT)


def _const_spec(shape):
    nd = len(shape)
    return pl.BlockSpec(shape, lambda *_: (0,) * nd, pipeline_mode=pl.Buffered(1))


def _tile_spec(width):
    return pl.BlockSpec((TM, width), lambda i: (i, 0))


def _prompt_tile_spec(width):
    return pl.BlockSpec((TM, width), lambda i: (jnp.minimum(i, N_PROMPT_TILES - 1), 0))


def _feature_major_index(i):
    j = jnp.minimum(i, N_PROMPT_TILES - 1)
    return (j // TILES_PER_SEQ, 0, j % TILES_PER_SEQ)


def _feature_major_spec():
    return pl.BlockSpec((1, B_WIDTH, TM), _feature_major_index)


def _layered_feature_major_spec(layer, n_slots):
    assert layer % n_slots == 0
    return pl.BlockSpec((n_slots, 1, B_WIDTH, TM),
                        lambda i: (layer // n_slots,) + _feature_major_index(i))


def _ffn_kernel(*refs, first, final, mixed):
    n_x = (2 if first else 1) + (4 if mixed else 0)
    x_refs, (g_ref, wg_ref, wu_ref, wd_ref, fg_ref) = refs[:n_x], refs[n_x:n_x + 5]
    n_scratch = 2 if mixed else 1
    o_refs, acc_ref = refs[n_x + 5:-n_scratch], refs[-1]
    i = pl.program_id(0)
    is_prompt = i < N_PROMPT_TILES
    if first:
        x = jnp.where(is_prompt, x_refs[0][...], x_refs[1][...])
    elif mixed:
        x_ref, ya_ref, ybt_ref, ybs_ref, wo_ref = x_refs
        xin_ref = refs[-2]
        base = x_ref[...] + jnp.dot(ya_ref[...], wo_ref[:A_WIDTH, :], preferred_element_type=f32)

        @pl.when(is_prompt)
        def _():
            xin_ref[...] = base + lax.dot_general(ybt_ref[0], wo_ref[A_WIDTH:, :], _TN,
                                                  preferred_element_type=f32)

        @pl.when(jnp.logical_not(is_prompt))
        def _():
            xin_ref[...] = base + jnp.dot(ybs_ref[...], wo_ref[A_WIDTH:, :],
                                          preferred_element_type=f32)

        x = xin_ref[...]
    else:
        x = x_refs[0][...]
    h = _rms(x, g_ref[...]).astype(bf16)
    for c in range(N_FF_CHUNKS):
        sl = slice(c * FF_CHUNK, (c + 1) * FF_CHUNK)
        gate = jnp.dot(h, wg_ref[:, sl].astype(bf16), preferred_element_type=f32)
        up = jnp.dot(h, wu_ref[:, sl].astype(bf16), preferred_element_type=f32)
        act = (gate / (1.0 + jnp.exp(-gate)) * up).astype(bf16)
        down = jnp.dot(act, wd_ref[sl, :].astype(bf16), preferred_element_type=f32)
        if c == 0:
            acc_ref[...] = down
        else:
            acc_ref[...] += down
    y = x + 0.5 * acc_ref[...]
    if final:
        y = _rms(y, fg_ref[...])

        @pl.when(is_prompt)
        def _():
            o_refs[0][...] = y

        @pl.when(jnp.logical_not(is_prompt))
        def _():
            o_refs[1][...] = y
    else:
        o_refs[0][...] = y


def _ffn(xs, g, wg, wu, wd, layer, fg, first=False, final=False, mix=()):
    def layer_spec(rows, cols):
        return pl.BlockSpec((None, rows, cols), lambda i: (layer, 0, 0),
                            pipeline_mode=pl.Buffered(1))

    sample_spec = pl.BlockSpec((TM, D_MODEL), lambda i: (0, 0))
    x_specs = [_prompt_tile_spec(D_MODEL), sample_spec] if first else [_tile_spec(D_MODEL)]
    scratch = [pltpu.VMEM((TM, D_MODEL), f32)]
    if mix:
        x_specs += [_tile_spec(A_WIDTH), _feature_major_spec(),
                    _const_spec((TM, B_WIDTH)), _const_spec((D_MODEL, D_MODEL))]
        scratch = [pltpu.VMEM((TM, D_MODEL), f32)] + scratch
    if final:
        out_specs = [_prompt_tile_spec(D_MODEL), sample_spec]
        out_shape = [jax.ShapeDtypeStruct((N_PROMPT, D_MODEL), f32),
                     jax.ShapeDtypeStruct((N_SAMPLE, D_MODEL), f32)]
    else:
        out_specs = [_tile_spec(D_MODEL)]
        out_shape = [jax.ShapeDtypeStruct((N_TOK, D_MODEL), f32)]
    return pl.pallas_call(
        functools.partial(_ffn_kernel, first=first, final=final, mixed=bool(mix)),
        grid=(N_TILES,),
        in_specs=x_specs + [_const_spec((1, D_MODEL)),
                            layer_spec(D_MODEL, D_FF), layer_spec(D_MODEL, D_FF),
                            layer_spec(D_FF, D_MODEL), _const_spec((1, D_MODEL))],
        out_specs=out_specs,
        out_shape=out_shape,
        scratch_shapes=scratch,
        compiler_params=_params(("arbitrary",)),
        name="ffn",
    )(*xs, *mix, g, wg, wu, wd, fg)


ROW_COLS = 2 * A_WIDTH + B_WIDTH
T_ROWS = 3 * B_WIDTH


def _even_in_kernel(*refs, n_prev):
    (x_ref, g_ref, w_ref, wt_ref, gain_ref, ws_ref, bs_ref) = refs[:7]
    (ya_ref, vs_ref, qt_ref, kt_ref, vt_ref, vtb_ref, kb_ref, km_ref, st_ref,
     z_ref, zt_ref) = refs[7 + n_prev:]
    i = pl.program_id(0)
    is_prompt = i < N_PROMPT_TILES
    h = _rms(x_ref[...], g_ref[...]).astype(bf16)
    z_ref[...] = jnp.dot(h, w_ref[...], preferred_element_type=f32)
    zt_ref[...] = lax.dot_general(wt_ref[...], h, _NT, preferred_element_type=f32)

    def gelu(a):
        return a * (0.5 * (1.0 + jnp.tanh(0.7978845608028654 * (a + 0.044715 * (a * a * a)))))

    row = lax.broadcasted_iota(jnp.int32, (TM, TM), 0)
    col = lax.broadcasted_iota(jnp.int32, (TM, TM), 1)
    chunk = lambda r: jnp.where(is_prompt, r >> _log2(A_CHUNK), r & (DEC_BATCH - 1))
    keep = (chunk(row) == chunk(col)) & (col <= row)
    for g in range(A_GROUPS):
        gs = slice(g * A_GROUP_DIM, (g + 1) * A_GROUP_DIM)
        u = gelu(z_ref[:, gs])
        v = gelu(z_ref[:, A_WIDTH + g * A_GROUP_DIM:A_WIDTH + (g + 1) * A_GROUP_DIM])
        vn = _rms(v, gain_ref[:, gs])
        w = jnp.where(keep, ws_ref[0, g], 0.0).astype(bf16)
        mixed = jnp.dot(w, vn.astype(bf16), preferred_element_type=f32) + bs_ref[0, g]
        ya_ref[:, gs] = (u * mixed).astype(bf16)

        @pl.when(i == N_TILES - 1)
        def _():
            vs_ref[:, gs] = vn

    k = z_ref[:, 2 * A_WIDTH:]
    kb_ref[...] = k.astype(bf16)
    for blk in range(BLOCKS_PER_TILE):
        ks = jnp.sum(k[blk * MOBA_BLOCK:(blk + 1) * MOBA_BLOCK], axis=0, keepdims=True)
        km_ref[0, blk:blk + 1, :] = ks * (1.0 / MOBA_BLOCK)

    @pl.when(is_prompt)
    def _():
        qt_ref[0] = zt_ref[:B_WIDTH, :]
        kt = zt_ref[B_WIDTH:2 * B_WIDTH, :]
        vt = zt_ref[2 * B_WIDTH:, :]
        for slot in range(kt_ref.shape[0]):
            kt_ref[slot, 0] = kt
            vt_ref[slot, 0] = vt
        vtb_ref[0] = vt.astype(bf16)

    @pl.when(i == N_TILES - 1)
    def _():
        for part in range(3):
            for t in range(DEC_SEQ):
                st_ref[part, t] = zt_ref[part * B_WIDTH:(part + 1) * B_WIDTH,
                                         t * DEC_BATCH:(t + 1) * DEC_BATCH]


def _even_in(x, g, w_row, w_t, gain, ws_tiled, bs_tiled, layer, kv_stacks):
    kind = lambda i: (i // N_PROMPT_TILES, 0, 0, 0)
    fm = lambda dt: jax.ShapeDtypeStruct((BATCH, B_WIDTH, SEQ), dt)
    stack = jax.ShapeDtypeStruct((N_EVEN, BATCH, B_WIDTH, SEQ), f32)
    n_in = 7
    stack_spec = _layered_feature_major_spec(layer, 1 if kv_stacks else N_EVEN)
    return pl.pallas_call(
        functools.partial(_even_in_kernel, n_prev=len(kv_stacks)),
        grid=(N_TILES,),
        in_specs=[_tile_spec(D_MODEL), _const_spec((1, D_MODEL)),
                  _const_spec((D_MODEL, ROW_COLS)), _const_spec((T_ROWS, D_MODEL)),
                  _const_spec((1, A_WIDTH)),
                  pl.BlockSpec((1, A_GROUPS, TM, TM), kind),
                  pl.BlockSpec((1, A_GROUPS, TM, 1), kind)]
        + [pl.BlockSpec(memory_space=pl.ANY)] * len(kv_stacks),
        out_specs=[_tile_spec(A_WIDTH),
                   pl.BlockSpec((TM, A_WIDTH), lambda i: (0, 0)),
                   _feature_major_spec(), stack_spec, stack_spec,
                   _feature_major_spec(),
                   _tile_spec(B_WIDTH),
                   pl.BlockSpec((1, BLOCKS_PER_TILE, B_WIDTH), lambda i: (i, 0, 0)),
                   pl.BlockSpec((3, DEC_SEQ, B_WIDTH, DEC_BATCH), lambda i: (0, 0, 0, 0))],
        out_shape=[jax.ShapeDtypeStruct((N_TOK, A_WIDTH), bf16),
                   jax.ShapeDtypeStruct((N_SAMPLE, A_WIDTH), f32),
                   fm(f32), stack, stack, fm(bf16),
                   jax.ShapeDtypeStruct((N_TOK, B_WIDTH), bf16),
                   jax.ShapeDtypeStruct((N_TILES, BLOCKS_PER_TILE, B_WIDTH), f32),
                   jax.ShapeDtypeStruct((3, DEC_SEQ, B_WIDTH, DEC_BATCH), f32)],
        scratch_shapes=[pltpu.VMEM((TM, ROW_COLS), f32), pltpu.VMEM((T_ROWS, TM), f32)],
        input_output_aliases={n_in + s: 3 + s for s in range(len(kv_stacks))},
        compiler_params=_params(("arbitrary",)),
        name="even_in",
    )(x, g, w_row, w_t, gain, ws_tiled, bs_tiled, *kv_stacks)


def _select_blocks(gate, n_valid, n_past, axis):
    blk = lax.broadcasted_iota(jnp.int32, gate.shape, axis)
    valid = blk < n_valid
    g = jnp.where(valid, gate, -jnp.inf)
    sel = jnp.zeros(gate.shape, jnp.bool_)
    for _ in range(min(MOBA_TOPK, n_past)):
        mx = jnp.max(g, axis=axis, keepdims=True)
        idx = jnp.min(jnp.where(g == mx, blk, n_past), axis=axis, keepdims=True)
        hit = blk == idx
        sel = sel | (hit & valid)
        g = jnp.where(hit, -jnp.inf, g)
    return sel


PAIR_COLS = 2 * MOBA_BLOCK
ALL_COLS = B_HEADS * MOBA_BLOCK


LOG2E = 1.4426950408889634


def _round_bf16(x):
    bits = struct.unpack("<I", struct.pack("<f", x))[0]
    bits = (bits + 0x7FFF + ((bits >> 16) & 1)) & 0xFFFF0000
    return struct.unpack("<f", struct.pack("<I", bits))[0]


def _bf16_terms(x, n):
    terms = []
    for _ in range(n):
        t = _round_bf16(x)
        terms.append(t)
        x -= t
    return terms


SLOPE_TERMS = 4
SUM_ROWS = 16
KEY_UNROLL = 2


def _moba_prompt_kernel(q_ref, k_ref, v_ref, km_ref, o_ref,
                        kpos_ref, qbd_ref, bias_ref, s0_ref, s1_ref, m_ref, acc_ref):
    b = pl.program_id(0)
    i = pl.program_id(1)

    @pl.when((b == 0) & (i == 0))
    def _():
        key = lax.broadcasted_iota(jnp.int32, (MOBA_BLOCK, HEAD_PAIR), 0)
        col = lax.broadcasted_iota(jnp.int32, (MOBA_BLOCK, HEAD_PAIR), 1)
        kpos_ref[...] = jnp.where(col < SLOPE_TERMS, key, 0).astype(bf16)
        r = lax.broadcasted_iota(jnp.int32, (HEAD_PAIR, PAIR_COLS), 0)
        e_of_col = lax.broadcasted_iota(jnp.int32, (HEAD_PAIR, PAIR_COLS), 1) >> _log2(MOBA_BLOCK)
        for p in range(N_HEAD_PAIRS):
            slope_rows = jnp.zeros((HEAD_PAIR, PAIR_COLS), f32)
            for e in range(2):
                for t, term in enumerate(_bf16_terms(SLOPES[2 * p + e] * LOG2E, SLOPE_TERMS)):
                    slope_rows = jnp.where((r == t) & (e_of_col == e), term, slope_rows)
            qbd_ref[p, HEAD_PAIR:, :] = slope_rows.astype(bf16)

    feat = lax.broadcasted_iota(jnp.int32, (HEAD_PAIR, PAIR_COLS), 0)
    qcol = lax.broadcasted_iota(jnp.int32, (HEAD_PAIR, PAIR_COLS), 1)
    mine = (feat >> _log2(B_HEAD_DIM)) == (qcol >> _log2(MOBA_BLOCK))
    blk = lax.broadcasted_iota(jnp.int32, (N_BLOCKS, PAIR_COLS), 0)
    bcol = lax.broadcasted_iota(jnp.int32, (N_BLOCKS, PAIR_COLS), 1)
    blocks_back = (i - blk).astype(f32) * float(MOBA_BLOCK)
    for p in range(N_HEAD_PAIRS):
        ls = slice(p * HEAD_PAIR, (p + 1) * HEAD_PAIR)
        q2 = q_ref[0, ls, :]
        qmf = jnp.where(mine, jnp.concatenate([q2, q2], axis=1), 0.0)
        qbd_ref[p, :HEAD_PAIR, :] = (qmf * (LOG2E * B_HEAD_DIM ** -0.5)).astype(bf16)
        gate = jnp.dot(km_ref[0, :, ls], qmf, precision=lax.Precision.HIGHEST,
                       preferred_element_type=f32)
        sel = _select_blocks(gate, i, N_BLOCKS - 1, 0)
        slope = jnp.where(bcol < MOBA_BLOCK, SLOPES[2 * p] * LOG2E, SLOPES[2 * p + 1] * LOG2E)
        bias = jnp.where(sel, -slope * blocks_back, NEG)
        bias_ref[:, p * PAIR_COLS:(p + 1) * PAIR_COLS] = jnp.where(blk == i, 0.0, bias)
    m_ref[...] = jnp.full((1, ALL_COLS), NEG, f32)
    acc_ref[...] = jnp.zeros((B_HEADS, B_HEAD_DIM + SUM_ROWS, MOBA_BLOCK), f32)

    def scores(n, s_ref, own):
        start = pl.multiple_of(n * MOBA_BLOCK, MOBA_BLOCK)
        for p in range(N_HEAD_PAIRS):
            kn = k_ref[pl.ds(start, MOBA_BLOCK), p * HEAD_PAIR:(p + 1) * HEAD_PAIR]
            kn = jnp.concatenate([kn, kpos_ref[...]], axis=1)
            s = jnp.dot(kn, qbd_ref[p], preferred_element_type=f32)
            if own:
                key = lax.broadcasted_iota(jnp.int32, (MOBA_BLOCK, PAIR_COLS), 0)
                query = lax.broadcasted_iota(jnp.int32, (MOBA_BLOCK, PAIR_COLS), 1) & (MOBA_BLOCK - 1)
                s = jnp.where(key <= query, s, NEG)
            s_ref[p] = s

    def absorb(n, s_ref):
        start = pl.multiple_of(n * MOBA_BLOCK, MOBA_BLOCK)
        for p in range(N_HEAD_PAIRS):
            cs = slice(p * PAIR_COLS, (p + 1) * PAIR_COLS)
            s = s_ref[p]
            brow = bias_ref[pl.ds(n, 1), cs]
            m_old = m_ref[:, cs]
            m_new = jnp.maximum(m_old, jnp.max(s, axis=0, keepdims=True) + brow)
            alpha = jnp.exp2(m_old - m_new)
            peb = jnp.exp2(s - (m_new - brow)).astype(bf16)
            m_ref[:, cs] = m_new
            for e in range(2):
                h = 2 * p + e
                es = slice(e * MOBA_BLOCK, (e + 1) * MOBA_BLOCK)
                vn = v_ref[0, h * B_HEAD_DIM:(h + 1) * B_HEAD_DIM, pl.ds(start, MOBA_BLOCK)]
                vn = jnp.concatenate([vn, jnp.ones((SUM_ROWS, MOBA_BLOCK), bf16)], axis=0)
                acc_ref[h] = alpha[:, es] * acc_ref[h] + jnp.dot(
                    vn, peb[:, es], preferred_element_type=f32)

    slots = (s0_ref, s1_ref)
    scores(i, s0_ref, True)

    def steps(first, count):
        for u in range(count):
            before = jnp.where(first == 0, i, first - 1) if u == 0 else first + u - 1
            scores(first + u, slots[(u + 1) % 2], False)
            absorb(before, slots[u % 2])

    def trip(jj, carry):
        steps(KEY_UNROLL * jj, KEY_UNROLL)
        return carry

    lax.fori_loop(0, i // KEY_UNROLL, trip, 0)
    done = (i // KEY_UNROLL) * KEY_UNROLL
    for rest in range(KEY_UNROLL):
        @pl.when(i - done == rest)
        def _():
            steps(done, rest)
            absorb(jnp.maximum(i - 1, 0), slots[rest % 2])

    for h in range(B_HEADS):
        o_ref[0, h * B_HEAD_DIM:(h + 1) * B_HEAD_DIM, :] = (
            acc_ref[h, :B_HEAD_DIM, :] / acc_ref[h, B_HEAD_DIM:B_HEAD_DIM + 1, :]).astype(bf16)


def _moba_prompt(q_t, kb, vtb, kmean):
    qspec = pl.BlockSpec((1, B_WIDTH, MOBA_BLOCK), lambda b, i: (b, 0, i))
    return pl.pallas_call(
        _moba_prompt_kernel,
        grid=(BATCH, N_BLOCKS),
        in_specs=[qspec,
                  pl.BlockSpec((SEQ, B_WIDTH), lambda b, i: (b, 0)),
                  pl.BlockSpec((1, B_WIDTH, SEQ), lambda b, i: (b, 0, 0)),
                  pl.BlockSpec((1, N_BLOCKS, B_WIDTH), lambda b, i: (b, 0, 0))],
        out_specs=qspec,
        out_shape=jax.ShapeDtypeStruct((BATCH, B_WIDTH, SEQ), bf16),
        scratch_shapes=[pltpu.VMEM((MOBA_BLOCK, HEAD_PAIR), bf16),
                        pltpu.VMEM((N_HEAD_PAIRS, 2 * HEAD_PAIR, PAIR_COLS), bf16),
                        pltpu.VMEM((N_BLOCKS, ALL_COLS), f32),
                        pltpu.VMEM((N_HEAD_PAIRS, MOBA_BLOCK, PAIR_COLS), f32),
                        pltpu.VMEM((N_HEAD_PAIRS, MOBA_BLOCK, PAIR_COLS), f32),
                        pltpu.VMEM((1, ALL_COLS), f32),
                        pltpu.VMEM((B_HEADS, B_HEAD_DIM + SUM_ROWS, MOBA_BLOCK), f32)],
        compiler_params=_params(("arbitrary", "arbitrary")),
        name="moba_prompt",
    )(q_t, kb, vtb, kmean)


SEQS_PER_STEP = 2


def _moba_sample_kernel(pt_ref, q_ref, kn_ref, vn_ref, *refs):
    seqs = range(SEQS_PER_STEP)
    k_refs = [refs[2 * u * N_PAGES:(2 * u + 1) * N_PAGES] for u in seqs]
    v_refs = [refs[(2 * u + 1) * N_PAGES:(2 * u + 2) * N_PAGES] for u in seqs]
    o_ref = refs[2 * SEQS_PER_STEP * N_PAGES]
    rows = DEC_SEQ * B_HEADS
    row = lax.broadcasted_iota(jnp.int32, (rows, B_WIDTH), 0)
    lane = lax.broadcasted_iota(jnp.int32, (rows, B_WIDTH), 1)
    head_lanes = (lane >> _log2(B_HEAD_DIM)) == (row & (B_HEADS - 1))
    r1 = lax.broadcasted_iota(jnp.int32, (rows, 1), 0)
    hrow = r1 & (B_HEADS - 1)
    slope = jnp.zeros((rows, 1), f32)
    for h in range(B_HEADS):
        slope = jnp.where(hrow == h, SLOPES[h], slope)
    tq = r1 >> _log2(B_HEADS)
    kcol = lax.broadcasted_iota(jnp.int32, (rows, PAGE_SIZE), 1)
    tk = lax.broadcasted_iota(jnp.int32, (rows, NEW_PAD), 1)
    flane = lax.broadcasted_iota(jnp.int32, (B_WIDTH, PAGE_SIZE), 1)

    qbd, raw, raw_new = [], [], []
    for u in seqs:
        qf = jnp.where(head_lanes, q_ref[u], 0.0)
        qb = (qf * (B_HEAD_DIM ** -0.5)).astype(bf16)
        qbd.append(qf)
        raw.append([jnp.dot(qb, k_refs[u][pg][...].astype(bf16), preferred_element_type=f32)
                    for pg in range(N_PAGES)])
        raw_new.append(lax.dot_general(qb, kn_ref[u].astype(bf16), _NT,
                                       preferred_element_type=f32))

    kmean = []
    for u in seqs:
        km = jnp.zeros((B_WIDTH, PAGE_SIZE), f32)
        for blk in range(N_PAST_BLOCKS):
            tot = k_refs[u][blk * PAGES_PER_BLOCK][...]
            for pg in range(blk * PAGES_PER_BLOCK + 1, (blk + 1) * PAGES_PER_BLOCK):
                tot = tot + k_refs[u][pg][...]
            mean = jnp.sum(tot, axis=1, keepdims=True) * (1.0 / MOBA_BLOCK)
            km = jnp.where(flane == blk, mean, km)
        kmean.append(km)
    gate = [jnp.dot(qbd[u], kmean[u], precision=lax.Precision.HIGHEST,
                    preferred_element_type=f32) for u in seqs]

    scores, s_new, m = [], [], []
    for u in seqs:
        sel = _select_blocks(gate[u], N_PAST_BLOCKS, N_PAST_BLOCKS, 1)
        blk_bias = jnp.where(sel, 0.0, NEG)
        sn = jnp.where(tk <= tq, raw_new[u] - slope * (tq - tk).astype(f32), NEG)
        mu = jnp.max(sn, axis=1, keepdims=True)
        su = []
        for pg in range(N_PAGES):
            dist = (tq + (PAST_LEN - pg * PAGE_SIZE) - kcol).astype(f32)
            blk = pg // PAGES_PER_BLOCK
            s = raw[u][pg] - slope * dist + blk_bias[:, blk:blk + 1]
            mu = jnp.maximum(mu, jnp.max(s, axis=1, keepdims=True))
            su.append(s)
        scores.append(su)
        s_new.append(sn)
        m.append(mu)

    for u in seqs:
        p_new = jnp.exp(s_new[u] - m[u])
        l = jnp.sum(p_new, axis=1, keepdims=True)
        acc = jnp.dot(p_new.astype(bf16), vn_ref[u].astype(bf16), preferred_element_type=f32)
        for pg in range(N_PAGES):
            pe = jnp.exp(scores[u][pg] - m[u])
            l = l + jnp.sum(pe, axis=1, keepdims=True)
            acc = acc + lax.dot_general(pe.astype(bf16), v_refs[u][pg][...].astype(bf16), _NT,
                                        preferred_element_type=f32)
        out = jnp.where(head_lanes, acc / l, 0.0)
        o_ref[u] = jnp.sum(out.reshape(DEC_SEQ, B_HEADS, B_WIDTH), axis=1)


def _moba_sample(page_table, q, k_new, v_new, cache_k, cache_v, layer):
    seq_spec = lambda r: pl.BlockSpec((SEQS_PER_STEP, r, B_WIDTH), lambda b, pt: (b, 0, 0))

    def page_spec(u, pg):
        return pl.BlockSpec(
            (None, None, B_WIDTH, PAGE_SIZE),
            lambda b, pt: (layer, pt[(b * SEQS_PER_STEP + u) * N_PAGES + pg], 0, 0))

    pages, operands = [], []
    for u in range(SEQS_PER_STEP):
        for cache in (cache_k, cache_v):
            pages += [page_spec(u, pg) for pg in range(N_PAGES)]
            operands += [cache] * N_PAGES
    return pl.pallas_call(
        _moba_sample_kernel,
        grid_spec=pltpu.PrefetchScalarGridSpec(
            num_scalar_prefetch=1,
            grid=(DEC_BATCH // SEQS_PER_STEP,),
            in_specs=[seq_spec(DEC_SEQ * B_HEADS), seq_spec(NEW_PAD), seq_spec(NEW_PAD)] + pages,
            out_specs=seq_spec(DEC_SEQ)),
        out_shape=jax.ShapeDtypeStruct((DEC_BATCH, DEC_SEQ, B_WIDTH), f32),
        compiler_params=_params(("parallel",)),
        name="moba_sample",
    )(page_table, q, k_new, v_new, *operands)


def _even_out_kernel(x_ref, ya_ref, ybt_ref, ybs_ref, w_ref, o_ref):
    i = pl.program_id(0)
    is_prompt = i < N_PROMPT_TILES
    base = x_ref[...] + jnp.dot(ya_ref[...], w_ref[:A_WIDTH, :], preferred_element_type=f32)

    @pl.when(is_prompt)
    def _():
        o_ref[...] = base + lax.dot_general(ybt_ref[0], w_ref[A_WIDTH:, :], _TN,
                                            preferred_element_type=f32)

    @pl.when(jnp.logical_not(is_prompt))
    def _():
        o_ref[...] = base + jnp.dot(ybs_ref[...], w_ref[A_WIDTH:, :], preferred_element_type=f32)


def _even_out(x, ya, yb_t, yb_s, w_out):
    return pl.pallas_call(
        _even_out_kernel,
        grid=(N_TILES,),
        in_specs=[_tile_spec(D_MODEL), _tile_spec(A_WIDTH), _feature_major_spec(),
                  _const_spec((TM, B_WIDTH)), _const_spec((D_MODEL, D_MODEL))],
        out_specs=_tile_spec(D_MODEL),
        out_shape=jax.ShapeDtypeStruct((N_TOK, D_MODEL), f32),
        compiler_params=_params(("arbitrary",)),
        name="even_out",
    )(x, ya, yb_t, yb_s, w_out)


def _odd_kernel(x_ref, g_ref, wi_ref, cw_ref, wo_ref, st_ref,
                o_ref, cp_ref, cs_ref, gated_ref, ext_ref):
    i = pl.program_id(0)
    is_prompt = i < N_PROMPT_TILES
    h = _rms(x_ref[...], g_ref[...]).astype(bf16)

    @pl.when(i % TILES_PER_SEQ == 0)
    def _():
        ext_ref[CONV_FRONT - SUBLANES:CONV_FRONT, :] = jnp.zeros((SUBLANES, C_WIDTH), f32)

    @pl.when(jnp.logical_not(is_prompt))
    def _():
        ext_ref[:CONV_FRONT, :] = st_ref[...]

    def project(part, c):
        lo = part * C_WIDTH + c * CONV_CHUNK
        return jnp.dot(h, wi_ref[:, lo:lo + CONV_CHUNK], preferred_element_type=f32)

    for c in range(C_WIDTH // CONV_CHUNK):
        cs = slice(c * CONV_CHUNK, (c + 1) * CONV_CHUNK)
        cx = project(2, c) * project(0, c)
        ext_ref[CONV_FRONT:, cs] = cx
        back1 = jnp.where(is_prompt, ext_ref[CONV_FRONT - 1:CONV_FRONT - 1 + TM, cs],
                          ext_ref[CONV_FRONT - DEC_BATCH:CONV_FRONT - DEC_BATCH + TM, cs])
        back2 = jnp.where(is_prompt, ext_ref[CONV_FRONT - 2:CONV_FRONT - 2 + TM, cs],
                          ext_ref[:TM, cs])
        conv = cw_ref[0:1, cs] * back2 + cw_ref[1:2, cs] * back1 + cw_ref[2:3, cs] * cx
        gated_ref[:, cs] = (project(1, c) * conv).astype(bf16)
    o_ref[...] = x_ref[...] + jnp.dot(gated_ref[...], wo_ref[...], preferred_element_type=f32)

    @pl.when(is_prompt)
    def _():
        last = ext_ref[CONV_FRONT + TM - SUBLANES:, :]
        ext_ref[CONV_FRONT - SUBLANES:CONV_FRONT, :] = last
        cp_ref[0] = last

    @pl.when(i == N_TILES - 1)
    def _():
        cs_ref[...] = ext_ref[TM:, :]


def _odd(x, g, w_in, conv_w, w_out, state):
    return pl.pallas_call(
        _odd_kernel,
        grid=(N_TILES,),
        in_specs=[_tile_spec(D_MODEL), _const_spec((1, D_MODEL)),
                  _const_spec((D_MODEL, 3 * C_WIDTH)), _const_spec((CONV_W, C_WIDTH)),
                  _const_spec((C_WIDTH, D_MODEL)), _const_spec((CONV_FRONT, C_WIDTH))],
        out_specs=[_tile_spec(D_MODEL),
                   pl.BlockSpec((1, SUBLANES, C_WIDTH),
                                lambda i: (jnp.minimum(i // TILES_PER_SEQ, BATCH - 1), 0, 0)),
                   pl.BlockSpec((CONV_FRONT, C_WIDTH), lambda i: (0, 0))],
        out_shape=[jax.ShapeDtypeStruct((N_TOK, D_MODEL), f32),
                   jax.ShapeDtypeStruct((BATCH, SUBLANES, C_WIDTH), f32),
                   jax.ShapeDtypeStruct((CONV_FRONT, C_WIDTH), f32)],
        scratch_shapes=[pltpu.VMEM((TM, C_WIDTH), bf16),
                        pltpu.VMEM((CONV_FRONT + TM, C_WIDTH), f32)],
        compiler_params=_params(("arbitrary",)),
        name="odd_mixer",
    )(x, g, w_in, conv_w, w_out, state)


def _step_major(a):
    return jnp.swapaxes(a, 0, 1).reshape(a.shape[0] * a.shape[1], a.shape[2])


def _batch_major(a, steps=DEC_SEQ):
    return jnp.swapaxes(a.reshape(steps, DEC_BATCH, a.shape[-1]), 0, 1)


def kernel(x_prompt, x_sample, cache_k, cache_v, state_conv, page_table, ffn1_norm, ffn1_gate, ffn1_up, ffn1_down, mix_norm, ffn2_norm, ffn2_gate, ffn2_up, ffn2_down, even_w_in, even_w_out, a_v_gain, a_w_s, a_b_s, odd_w_in, odd_conv_w, odd_w_out, final_norm):
    n_phys = cache_k.shape[1]
    pages = lambda c: jnp.transpose(c, (0, 1, 3, 4, 2)).reshape(N_EVEN, n_phys, B_WIDTH, PAGE_SIZE)
    ck, cv = pages(cache_k), pages(cache_v)
    pt = page_table.reshape(-1).astype(jnp.int32)
    row = lambda a: a.reshape(1, -1)
    fg = row(final_norm)
    xs = (x_prompt.reshape(N_PROMPT, D_MODEL), _step_major(x_sample))

    k_s, v_s, conv_p, conv_s, av_s = [], [], [], [], []
    kv_stacks = ()
    for i in range(DEPTH):
        j = i // 2
        (x,) = _ffn(xs, row(ffn1_norm[i]), ffn1_gate, ffn1_up, ffn1_down, i, fg, first=(i == 0))
        if i % 2 == 0:
            reps = TM // A_CHUNK
            w4 = a_w_s[j][:, :DEC_SEQ, :DEC_SEQ]
            ws_tiled = jnp.stack([
                jnp.tile(a_w_s[j], (1, reps, reps)),
                jnp.repeat(jnp.repeat(w4, DEC_BATCH, axis=1), DEC_BATCH, axis=2)])
            bs_tiled = jnp.stack([
                jnp.tile(a_b_s[j], (1, reps)),
                jnp.repeat(a_b_s[j][:, :DEC_SEQ], DEC_BATCH, axis=1)])[..., None]
            w_in = even_w_in[j].astype(bf16)
            w_row = jnp.concatenate([w_in[:, :2 * A_WIDTH],
                                     w_in[:, 2 * A_WIDTH + B_WIDTH:2 * A_WIDTH + 2 * B_WIDTH]], axis=1)
            w_t = w_in[:, 2 * A_WIDTH:].T
            ya, va, q_t, k_stack, v_stack, vtb, kb, km, st = _even_in(
                x, row(mix_norm[i]), w_row, w_t, row(a_v_gain[j]), ws_tiled, bs_tiled, j, kv_stacks)
            kv_stacks = (k_stack, v_stack)
            kmean = km.reshape(N_TOK // MOBA_BLOCK, B_WIDTH)[:BATCH * N_BLOCKS]
            yb_t = _moba_prompt(q_t, kb, vtb, kmean.reshape(BATCH, N_BLOCKS, B_WIDTH))
            per_seq = jnp.transpose(st, (0, 3, 1, 2))
            padn = lambda a: jnp.pad(a, ((0, 0), (0, NEW_PAD - DEC_SEQ), (0, 0)))
            q_rep = jnp.repeat(per_seq[0], B_HEADS, axis=1)
            yb_s = _moba_sample(pt, q_rep, padn(per_seq[1]), padn(per_seq[2]), ck, cv, j)
            mix = (ya, yb_t, _step_major(yb_s).astype(bf16), even_w_out[j].astype(bf16))
            k_s.append(st[1].reshape(DEC_SEQ, B_HEADS, B_HEAD_DIM, DEC_BATCH))
            v_s.append(st[2].reshape(DEC_SEQ, B_HEADS, B_HEAD_DIM, DEC_BATCH))
            av_s.append(_batch_major(va))
        else:
            state = jnp.swapaxes(state_conv[j], 0, 1).reshape(CONV_FRONT, C_WIDTH)
            x, cp, cs = _odd(x, row(mix_norm[i]), odd_w_in[j].astype(bf16), odd_conv_w[j],
                             odd_w_out[j].astype(bf16), state)
            conv_p.append(cp[:, SUBLANES - (CONV_W - 1):])
            conv_s.append(_batch_major(cs, CONV_W - 1))
            mix = ()
        xs = _ffn((x,), row(ffn2_norm[i]), ffn2_gate, ffn2_up, ffn2_down, i, fg,
                  final=(i == DEPTH - 1), mix=mix)

    y_prompt = xs[0].reshape(BATCH, SEQ, D_MODEL)
    y_sample = _batch_major(xs[1])
    to_rows_p = lambda a: jnp.transpose(
        a.reshape(N_EVEN, BATCH, B_HEADS, B_HEAD_DIM, SEQ), (0, 1, 4, 2, 3))
    to_rows_s = lambda ts: jnp.transpose(jnp.stack(ts), (0, 4, 1, 2, 3))
    return (y_prompt, y_sample, to_rows_p(kv_stacks[0]), to_rows_p(kv_stacks[1]),
            jnp.stack(conv_p),
            to_rows_s(k_s), to_rows_s(v_s), jnp.stack(conv_s), jnp.stack(av_s))
```

```python
import functools
import struct

import jax
import jax.numpy as jnp
from jax import lax
from jax.experimental import pallas as pl
from jax.experimental.pallas import tpu as pltpu

D_MODEL = 1024
BATCH = 4
SEQ = 4096
DEPTH = 4
DEC_BATCH = 128
DEC_SEQ = 4
PAST_LEN = 2048
PAGE_SIZE = 128
N_EVEN = (DEPTH + 1) // 2
N_ODD = DEPTH // 2
D_FF = 2816
RMS_EPS = 1e-6
A_WIDTH = D_MODEL // 2
A_CHUNK = 128
A_GROUPS = 4
A_GROUP_DIM = A_WIDTH // A_GROUPS
B_HEAD_DIM = 64
B_WIDTH = D_MODEL - A_WIDTH
B_HEADS = B_WIDTH // B_HEAD_DIM
MOBA_BLOCK = 256
MOBA_TOPK = 3
C_WIDTH = D_MODEL
CONV_W = 3

N_PROMPT = BATCH * SEQ
N_SAMPLE = DEC_BATCH * DEC_SEQ
N_TOK = N_PROMPT + N_SAMPLE
TM = 512
N_TILES = N_TOK // TM
N_PROMPT_TILES = N_PROMPT // TM
TILES_PER_SEQ = SEQ // TM
FF_CHUNK = 256
N_FF_CHUNKS = D_FF // FF_CHUNK
N_PAGES = PAST_LEN // PAGE_SIZE
N_BLOCKS = SEQ // MOBA_BLOCK
N_PAST_BLOCKS = PAST_LEN // MOBA_BLOCK
PAGES_PER_BLOCK = MOBA_BLOCK // PAGE_SIZE
BLOCKS_PER_TILE = TM // MOBA_BLOCK
NEW_PAD = 16
HEAD_PAIR = 2 * B_HEAD_DIM
N_HEAD_PAIRS = B_HEADS // 2
SLOPES = tuple(2.0 ** (-8.0 * (h + 1) / B_HEADS) for h in range(B_HEADS))
NEG = -1e30
SUBLANES = 8
LANES = 128
CONV_FRONT = 2 * DEC_BATCH
CONV_CHUNK = 256
VMEM_LIMIT = 56 * 1024 * 1024

assert N_SAMPLE == TM and N_TOK % TM == 0 and D_FF % FF_CHUNK == 0
assert DEC_BATCH == LANES and CONV_W - 1 == 2 and DEC_SEQ >= CONV_W - 1

_NT = (((1,), (1,)), ((), ()))
_TN = (((0,), (0,)), ((), ()))
bf16 = jnp.bfloat16
f32 = jnp.float32


def _log2(n):
    assert n & (n - 1) == 0
    return n.bit_length() - 1


def _rms(x, g):
    ms = jnp.mean(x * x, axis=-1, keepdims=True)
    return x * lax.rsqrt(ms + RMS_EPS) * g


def _params(sem):
    return pltpu.CompilerParams(dimension_semantics=sem, vmem_limit_bytes=VMEM_LIMIT)


def _const_spec(shape):
    nd = len(shape)
    return pl.BlockSpec(shape, lambda *_: (0,) * nd, pipeline_mode=pl.Buffered(1))


def _tile_spec(width):
    return pl.BlockSpec((TM, width), lambda i: (i, 0))


def _prompt_tile_spec(width):
    return pl.BlockSpec((TM, width), lambda i: (jnp.minimum(i, N_PROMPT_TILES - 1), 0))


def _feature_major_index(i):
    j = jnp.minimum(i, N_PROMPT_TILES - 1)
    return (j // TILES_PER_SEQ, 0, j % TILES_PER_SEQ)


def _feature_major_spec():
    return pl.BlockSpec((1, B_WIDTH, TM), _feature_major_index)


def _layered_feature_major_spec(layer, n_slots):
    assert layer % n_slots == 0
    return pl.BlockSpec((n_slots, 1, B_WIDTH, TM),
                        lambda i: (layer // n_slots,) + _feature_major_index(i))


def _ffn_kernel(*refs, first, final, mixed, layer):
    n_x = (2 if first else 1) + (4 if mixed else 0)
    x_refs, (g_ref, wg_hbm, wu_hbm, wd_hbm, fg_ref) = refs[:n_x], refs[n_x:n_x + 5]
    n_out = 2 if final else 1
    o_refs = refs[n_x + 5:n_x + 5 + n_out]
    scratch = list(refs[n_x + 5 + n_out:])
    xin_ref = scratch.pop(0) if mixed else None
    acc_ref, wg_ref, wu_ref, wd_ref, sem = scratch
    i = pl.program_id(0)
    is_prompt = i < N_PROMPT_TILES

    def chunk_copies(c):
        sl = slice(c * FF_CHUNK, (c + 1) * FF_CHUNK)
        return (pltpu.make_async_copy(wg_hbm.at[layer, :, sl], wg_ref.at[:, sl], sem.at[0, c]),
                pltpu.make_async_copy(wu_hbm.at[layer, :, sl], wu_ref.at[:, sl], sem.at[1, c]),
                pltpu.make_async_copy(wd_hbm.at[layer, sl, :], wd_ref.at[sl, :], sem.at[2, c]))

    @pl.when(i == 0)
    def _():
        for c in range(N_FF_CHUNKS):
            for copy in chunk_copies(c):
                copy.start()

    if first:
        x = jnp.where(is_prompt, x_refs[0][...], x_refs[1][...])
    elif mixed:
        x_ref, ya_ref, ybt_ref, ybs_ref, wo_ref = x_refs
        base = x_ref[...] + jnp.dot(ya_ref[...], wo_ref[:A_WIDTH, :], preferred_element_type=f32)

        @pl.when(is_prompt)
        def _():
            xin_ref[...] = base + lax.dot_general(ybt_ref[0], wo_ref[A_WIDTH:, :], _TN,
                                                  preferred_element_type=f32)

        @pl.when(jnp.logical_not(is_prompt))
        def _():
            xin_ref[...] = base + jnp.dot(ybs_ref[...], wo_ref[A_WIDTH:, :],
                                          preferred_element_type=f32)

        x = xin_ref[...]
    else:
        x = x_refs[0][...]

    def body(wait_for_weights):
        h = _rms(x, g_ref[...]).astype(bf16)
        for c in range(N_FF_CHUNKS):
            sl = slice(c * FF_CHUNK, (c + 1) * FF_CHUNK)
            if wait_for_weights:
                for copy in chunk_copies(c):
                    copy.wait()
            gate = jnp.dot(h, wg_ref[:, sl].astype(bf16), preferred_element_type=f32)
            up = jnp.dot(h, wu_ref[:, sl].astype(bf16), preferred_element_type=f32)
            act = (gate / (1.0 + jnp.exp(-gate)) * up).astype(bf16)
            down = jnp.dot(act, wd_ref[sl, :].astype(bf16), preferred_element_type=f32)
            if c == 0:
                acc_ref[...] = down
            else:
                acc_ref[...] += down
        y = x + 0.5 * acc_ref[...]
        if final:
            y = _rms(y, fg_ref[...])

            @pl.when(is_prompt)
            def _():
                o_refs[0][...] = y

            @pl.when(jnp.logical_not(is_prompt))
            def _():
                o_refs[1][...] = y
        else:
            o_refs[0][...] = y

    @pl.when(i == 0)
    def _():
        body(True)

    @pl.when(i != 0)
    def _():
        body(False)


def _ffn(xs, g, wg, wu, wd, layer, fg, first=False, final=False, mix=()):
    in_hbm = pl.BlockSpec(memory_space=pl.ANY)
    sample_spec = pl.BlockSpec((TM, D_MODEL), lambda i: (0, 0))
    x_specs = [_prompt_tile_spec(D_MODEL), sample_spec] if first else [_tile_spec(D_MODEL)]
    scratch = [pltpu.VMEM((TM, D_MODEL), f32),
               pltpu.VMEM((D_MODEL, D_FF), f32), pltpu.VMEM((D_MODEL, D_FF), f32),
               pltpu.VMEM((D_FF, D_MODEL), f32),
               pltpu.SemaphoreType.DMA((3, N_FF_CHUNKS))]
    if mix:
        x_specs += [_tile_spec(A_WIDTH), _feature_major_spec(),
                    _const_spec((TM, B_WIDTH)), _const_spec((D_MODEL, D_MODEL))]
        scratch = [pltpu.VMEM((TM, D_MODEL), f32)] + scratch
    if final:
        out_specs = [_prompt_tile_spec(D_MODEL), sample_spec]
        out_shape = [jax.ShapeDtypeStruct((N_PROMPT, D_MODEL), f32),
                     jax.ShapeDtypeStruct((N_SAMPLE, D_MODEL), f32)]
    else:
        out_specs = [_tile_spec(D_MODEL)]
        out_shape = [jax.ShapeDtypeStruct((N_TOK, D_MODEL), f32)]
    return pl.pallas_call(
        functools.partial(_ffn_kernel, first=first, final=final, mixed=bool(mix), layer=layer),
        grid=(N_TILES,),
        in_specs=x_specs + [_const_spec((1, D_MODEL)), in_hbm, in_hbm, in_hbm,
                            _const_spec((1, D_MODEL))],
        out_specs=out_specs,
        out_shape=out_shape,
        scratch_shapes=scratch,
        compiler_params=_params(("arbitrary",)),
        name="ffn",
    )(*xs, *mix, g, wg, wu, wd, fg)


ROW_COLS = 2 * A_WIDTH + B_WIDTH
T_ROWS = 3 * B_WIDTH


def _even_in_kernel(*refs, n_prev):
    (x_ref, g_ref, w_ref, wt_ref, gain_ref, ws_ref, bs_ref) = refs[:7]
    (ya_ref, vs_ref, qt_ref, kt_ref, vt_ref, vtb_ref, kb_ref, km_ref, st_ref,
     z_ref, zt_ref) = refs[7 + n_prev:]
    i = pl.program_id(0)
    is_prompt = i < N_PROMPT_TILES
    h = _rms(x_ref[...], g_ref[...]).astype(bf16)
    z_ref[...] = jnp.dot(h, w_ref[...], preferred_element_type=f32)
    zt_ref[...] = lax.dot_general(wt_ref[...], h, _NT, preferred_element_type=f32)

    def gelu(a):
        return a * (0.5 * (1.0 + jnp.tanh(0.7978845608028654 * (a + 0.044715 * (a * a * a)))))

    row = lax.broadcasted_iota(jnp.int32, (TM, TM), 0)
    col = lax.broadcasted_iota(jnp.int32, (TM, TM), 1)
    chunk = lambda r: jnp.where(is_prompt, r >> _log2(A_CHUNK), r & (DEC_BATCH - 1))
    keep = (chunk(row) == chunk(col)) & (col <= row)
    for g in range(A_GROUPS):
        gs = slice(g * A_GROUP_DIM, (g + 1) * A_GROUP_DIM)
        u = gelu(z_ref[:, gs])
        v = gelu(z_ref[:, A_WIDTH + g * A_GROUP_DIM:A_WIDTH + (g + 1) * A_GROUP_DIM])
        vn = _rms(v, gain_ref[:, gs])
        w = jnp.where(keep, ws_ref[0, g], 0.0).astype(bf16)
        mixed = jnp.dot(w, vn.astype(bf16), preferred_element_type=f32) + bs_ref[0, g]
        ya_ref[:, gs] = (u * mixed).astype(bf16)

        @pl.when(i == N_TILES - 1)
        def _():
            vs_ref[:, gs] = vn

    k = z_ref[:, 2 * A_WIDTH:]
    kb_ref[...] = k.astype(bf16)
    for blk in range(BLOCKS_PER_TILE):
        ks = jnp.sum(k[blk * MOBA_BLOCK:(blk + 1) * MOBA_BLOCK], axis=0, keepdims=True)
        km_ref[0, blk:blk + 1, :] = ks * (1.0 / MOBA_BLOCK)

    @pl.when(is_prompt)
    def _():
        qt_ref[0] = zt_ref[:B_WIDTH, :]
        kt = zt_ref[B_WIDTH:2 * B_WIDTH, :]
        vt = zt_ref[2 * B_WIDTH:, :]
        for slot in range(kt_ref.shape[0]):
            kt_ref[slot, 0] = kt
            vt_ref[slot, 0] = vt
        vtb_ref[0] = vt.astype(bf16)

    @pl.when(i == N_TILES - 1)
    def _():
        for part in range(3):
            for t in range(DEC_SEQ):
                st_ref[part, t] = zt_ref[part * B_WIDTH:(part + 1) * B_WIDTH,
                                         t * DEC_BATCH:(t + 1) * DEC_BATCH]


def _even_in(x, g, w_row, w_t, gain, ws_tiled, bs_tiled, layer, kv_stacks):
    kind = lambda i: (i // N_PROMPT_TILES, 0, 0, 0)
    fm = lambda dt: jax.ShapeDtypeStruct((BATCH, B_WIDTH, SEQ), dt)
    stack = jax.ShapeDtypeStruct((N_EVEN, BATCH, B_WIDTH, SEQ), f32)
    n_in = 7
    stack_spec = _layered_feature_major_spec(layer, 1 if kv_stacks else N_EVEN)
    return pl.pallas_call(
        functools.partial(_even_in_kernel, n_prev=len(kv_stacks)),
        grid=(N_TILES,),
        in_specs=[_tile_spec(D_MODEL), _const_spec((1, D_MODEL)),
                  _const_spec((D_MODEL, ROW_COLS)), _const_spec((T_ROWS, D_MODEL)),
                  _const_spec((1, A_WIDTH)),
                  pl.BlockSpec((1, A_GROUPS, TM, TM), kind),
                  pl.BlockSpec((1, A_GROUPS, TM, 1), kind)]
        + [pl.BlockSpec(memory_space=pl.ANY)] * len(kv_stacks),
        out_specs=[_tile_spec(A_WIDTH),
                   pl.BlockSpec((TM, A_WIDTH), lambda i: (0, 0)),
                   _feature_major_spec(), stack_spec, stack_spec,
                   _feature_major_spec(),
                   _tile_spec(B_WIDTH),
                   pl.BlockSpec((1, BLOCKS_PER_TILE, B_WIDTH), lambda i: (i, 0, 0)),
                   pl.BlockSpec((3, DEC_SEQ, B_WIDTH, DEC_BATCH), lambda i: (0, 0, 0, 0))],
        out_shape=[jax.ShapeDtypeStruct((N_TOK, A_WIDTH), bf16),
                   jax.ShapeDtypeStruct((N_SAMPLE, A_WIDTH), f32),
                   fm(f32), stack, stack, fm(bf16),
                   jax.ShapeDtypeStruct((N_TOK, B_WIDTH), bf16),
                   jax.ShapeDtypeStruct((N_TILES, BLOCKS_PER_TILE, B_WIDTH), f32),
                   jax.ShapeDtypeStruct((3, DEC_SEQ, B_WIDTH, DEC_BATCH), f32)],
        scratch_shapes=[pltpu.VMEM((TM, ROW_COLS), f32), pltpu.VMEM((T_ROWS, TM), f32)],
        input_output_aliases={n_in + s: 3 + s for s in range(len(kv_stacks))},
        compiler_params=_params(("arbitrary",)),
        name="even_in",
    )(x, g, w_row, w_t, gain, ws_tiled, bs_tiled, *kv_stacks)


def _select_blocks(gate, n_valid, n_past, axis):
    blk = lax.broadcasted_iota(jnp.int32, gate.shape, axis)
    valid = blk < n_valid
    g = jnp.where(valid, gate, -jnp.inf)
    sel = jnp.zeros(gate.shape, jnp.bool_)
    for _ in range(min(MOBA_TOPK, n_past)):
        mx = jnp.max(g, axis=axis, keepdims=True)
        idx = jnp.min(jnp.where(g == mx, blk, n_past), axis=axis, keepdims=True)
        hit = blk == idx
        sel = sel | (hit & valid)
        g = jnp.where(hit, -jnp.inf, g)
    return sel


PAIR_COLS = 2 * MOBA_BLOCK
ALL_COLS = B_HEADS * MOBA_BLOCK


LOG2E = 1.4426950408889634


def _round_bf16(x):
    bits = struct.unpack("<I", struct.pack("<f", x))[0]
    bits = (bits + 0x7FFF + ((bits >> 16) & 1)) & 0xFFFF0000
    return struct.unpack("<f", struct.pack("<I", bits))[0]


def _bf16_terms(x, n):
    terms = []
    for _ in range(n):
        t = _round_bf16(x)
        terms.append(t)
        x -= t
    return terms


SLOPE_TERMS = 4
SUM_ROWS = 16
KEY_UNROLL = 2


def _moba_prompt_kernel(q_ref, k_ref, v_ref, km_ref, o_ref,
                        kpos_ref, qbd_ref, bias_ref, s0_ref, s1_ref, m_ref, acc_ref):
    b = pl.program_id(0)
    i = pl.program_id(1)

    @pl.when((b == 0) & (i == 0))
    def _():
        key = lax.broadcasted_iota(jnp.int32, (MOBA_BLOCK, HEAD_PAIR), 0)
        col = lax.broadcasted_iota(jnp.int32, (MOBA_BLOCK, HEAD_PAIR), 1)
        kpos_ref[...] = jnp.where(col < SLOPE_TERMS, key, 0).astype(bf16)
        r = lax.broadcasted_iota(jnp.int32, (HEAD_PAIR, PAIR_COLS), 0)
        e_of_col = lax.broadcasted_iota(jnp.int32, (HEAD_PAIR, PAIR_COLS), 1) >> _log2(MOBA_BLOCK)
        for p in range(N_HEAD_PAIRS):
            slope_rows = jnp.zeros((HEAD_PAIR, PAIR_COLS), f32)
            for e in range(2):
                for t, term in enumerate(_bf16_terms(SLOPES[2 * p + e] * LOG2E, SLOPE_TERMS)):
                    slope_rows = jnp.where((r == t) & (e_of_col == e), term, slope_rows)
            qbd_ref[p, HEAD_PAIR:, :] = slope_rows.astype(bf16)

    feat = lax.broadcasted_iota(jnp.int32, (HEAD_PAIR, PAIR_COLS), 0)
    qcol = lax.broadcasted_iota(jnp.int32, (HEAD_PAIR, PAIR_COLS), 1)
    mine = (feat >> _log2(B_HEAD_DIM)) == (qcol >> _log2(MOBA_BLOCK))
    blk = lax.broadcasted_iota(jnp.int32, (N_BLOCKS, PAIR_COLS), 0)
    bcol = lax.broadcasted_iota(jnp.int32, (N_BLOCKS, PAIR_COLS), 1)
    blocks_back = (i - blk).astype(f32) * float(MOBA_BLOCK)
    for p in range(N_HEAD_PAIRS):
        ls = slice(p * HEAD_PAIR, (p + 1) * HEAD_PAIR)
        q2 = q_ref[0, ls, :]
        qmf = jnp.where(mine, jnp.concatenate([q2, q2], axis=1), 0.0)
        qbd_ref[p, :HEAD_PAIR, :] = (qmf * (LOG2E * B_HEAD_DIM ** -0.5)).astype(bf16)
        gate = jnp.dot(km_ref[0, :, ls], qmf, precision=lax.Precision.HIGHEST,
                       preferred_element_type=f32)
        sel = _select_blocks(gate, i, N_BLOCKS - 1, 0)
        slope = jnp.where(bcol < MOBA_BLOCK, SLOPES[2 * p] * LOG2E, SLOPES[2 * p + 1] * LOG2E)
        bias = jnp.where(sel, -slope * blocks_back, NEG)
        bias_ref[:, p * PAIR_COLS:(p + 1) * PAIR_COLS] = jnp.where(blk == i, 0.0, bias)
    m_ref[...] = jnp.full((1, ALL_COLS), NEG, f32)
    acc_ref[...] = jnp.zeros((B_HEADS, B_HEAD_DIM + SUM_ROWS, MOBA_BLOCK), f32)

    def scores(n, s_ref, own):
        start = pl.multiple_of(n * MOBA_BLOCK, MOBA_BLOCK)
        for p in range(N_HEAD_PAIRS):
            kn = k_ref[pl.ds(start, MOBA_BLOCK), p * HEAD_PAIR:(p + 1) * HEAD_PAIR]
            kn = jnp.concatenate([kn, kpos_ref[...]], axis=1)
            s = jnp.dot(kn, qbd_ref[p], preferred_element_type=f32)
            if own:
                key = lax.broadcasted_iota(jnp.int32, (MOBA_BLOCK, PAIR_COLS), 0)
                query = lax.broadcasted_iota(jnp.int32, (MOBA_BLOCK, PAIR_COLS), 1) & (MOBA_BLOCK - 1)
                s = jnp.where(key <= query, s, NEG)
            s_ref[p] = s

    def absorb(n, s_ref):
        start = pl.multiple_of(n * MOBA_BLOCK, MOBA_BLOCK)
        for p in range(N_HEAD_PAIRS):
            cs = slice(p * PAIR_COLS, (p + 1) * PAIR_COLS)
            s = s_ref[p]
            brow = bias_ref[pl.ds(n, 1), cs]
            m_old = m_ref[:, cs]
            m_new = jnp.maximum(m_old, jnp.max(s, axis=0, keepdims=True) + brow)
            alpha = jnp.exp2(m_old - m_new)
            peb = jnp.exp2(s - (m_new - brow)).astype(bf16)
            m_ref[:, cs] = m_new
            for e in range(2):
                h = 2 * p + e
                es = slice(e * MOBA_BLOCK, (e + 1) * MOBA_BLOCK)
                vn = v_ref[0, h * B_HEAD_DIM:(h + 1) * B_HEAD_DIM, pl.ds(start, MOBA_BLOCK)]
                vn = jnp.concatenate([vn, jnp.ones((SUM_ROWS, MOBA_BLOCK), bf16)], axis=0)
                acc_ref[h] = alpha[:, es] * acc_ref[h] + jnp.dot(
                    vn, peb[:, es], preferred_element_type=f32)

    slots = (s0_ref, s1_ref)
    scores(i, s0_ref, True)

    def steps(first, count):
        for u in range(count):
            before = jnp.where(first == 0, i, first - 1) if u == 0 else first + u - 1
            scores(first + u, slots[(u + 1) % 2], False)
            absorb(before, slots[u % 2])

    def trip(jj, carry):
        steps(KEY_UNROLL * jj, KEY_UNROLL)
        return carry

    lax.fori_loop(0, i // KEY_UNROLL, trip, 0)
    done = (i // KEY_UNROLL) * KEY_UNROLL
    for rest in range(KEY_UNROLL):
        @pl.when(i - done == rest)
        def _():
            steps(done, rest)
            absorb(jnp.maximum(i - 1, 0), slots[rest % 2])

    for h in range(B_HEADS):
        o_ref[0, h * B_HEAD_DIM:(h + 1) * B_HEAD_DIM, :] = (
            acc_ref[h, :B_HEAD_DIM, :] / acc_ref[h, B_HEAD_DIM:B_HEAD_DIM + 1, :]).astype(bf16)


def _moba_prompt(q_t, kb, vtb, kmean):
    qspec = pl.BlockSpec((1, B_WIDTH, MOBA_BLOCK), lambda b, i: (b, 0, i))
    return pl.pallas_call(
        _moba_prompt_kernel,
        grid=(BATCH, N_BLOCKS),
        in_specs=[qspec,
                  pl.BlockSpec((SEQ, B_WIDTH), lambda b, i: (b, 0)),
                  pl.BlockSpec((1, B_WIDTH, SEQ), lambda b, i: (b, 0, 0)),
                  pl.BlockSpec((1, N_BLOCKS, B_WIDTH), lambda b, i: (b, 0, 0))],
        out_specs=qspec,
        out_shape=jax.ShapeDtypeStruct((BATCH, B_WIDTH, SEQ), bf16),
        scratch_shapes=[pltpu.VMEM((MOBA_BLOCK, HEAD_PAIR), bf16),
                        pltpu.VMEM((N_HEAD_PAIRS, 2 * HEAD_PAIR, PAIR_COLS), bf16),
                        pltpu.VMEM((N_BLOCKS, ALL_COLS), f32),
                        pltpu.VMEM((N_HEAD_PAIRS, MOBA_BLOCK, PAIR_COLS), f32),
                        pltpu.VMEM((N_HEAD_PAIRS, MOBA_BLOCK, PAIR_COLS), f32),
                        pltpu.VMEM((1, ALL_COLS), f32),
                        pltpu.VMEM((B_HEADS, B_HEAD_DIM + SUM_ROWS, MOBA_BLOCK), f32)],
        compiler_params=_params(("arbitrary", "arbitrary")),
        name="moba_prompt",
    )(q_t, kb, vtb, kmean)


SEQS_PER_STEP = 2


def _moba_sample_kernel(pt_ref, q_ref, kn_ref, vn_ref, *refs):
    seqs = range(SEQS_PER_STEP)
    k_refs = [refs[2 * u * N_PAGES:(2 * u + 1) * N_PAGES] for u in seqs]
    v_refs = [refs[(2 * u + 1) * N_PAGES:(2 * u + 2) * N_PAGES] for u in seqs]
    o_ref = refs[2 * SEQS_PER_STEP * N_PAGES]
    rows = DEC_SEQ * B_HEADS
    row = lax.broadcasted_iota(jnp.int32, (rows, B_WIDTH), 0)
    lane = lax.broadcasted_iota(jnp.int32, (rows, B_WIDTH), 1)
    head_lanes = (lane >> _log2(B_HEAD_DIM)) == (row & (B_HEADS - 1))
    r1 = lax.broadcasted_iota(jnp.int32, (rows, 1), 0)
    hrow = r1 & (B_HEADS - 1)
    slope = jnp.zeros((rows, 1), f32)
    for h in range(B_HEADS):
        slope = jnp.where(hrow == h, SLOPES[h], slope)
    tq = r1 >> _log2(B_HEADS)
    kcol = lax.broadcasted_iota(jnp.int32, (rows, PAGE_SIZE), 1)
    tk = lax.broadcasted_iota(jnp.int32, (rows, NEW_PAD), 1)
    flane = lax.broadcasted_iota(jnp.int32, (B_WIDTH, PAGE_SIZE), 1)

    qbd, raw, raw_new = [], [], []
    for u in seqs:
        qf = jnp.where(head_lanes, q_ref[u], 0.0)
        qb = (qf * (B_HEAD_DIM ** -0.5)).astype(bf16)
        qbd.append(qf)
        raw.append([jnp.dot(qb, k_refs[u][pg][...].astype(bf16), preferred_element_type=f32)
                    for pg in range(N_PAGES)])
        raw_new.append(lax.dot_general(qb, kn_ref[u].astype(bf16), _NT,
                                       preferred_element_type=f32))

    kmean = []
    for u in seqs:
        km = jnp.zeros((B_WIDTH, PAGE_SIZE), f32)
        for blk in range(N_PAST_BLOCKS):
            tot = k_refs[u][blk * PAGES_PER_BLOCK][...]
            for pg in range(blk * PAGES_PER_BLOCK + 1, (blk + 1) * PAGES_PER_BLOCK):
                tot = tot + k_refs[u][pg][...]
            mean = jnp.sum(tot, axis=1, keepdims=True) * (1.0 / MOBA_BLOCK)
            km = jnp.where(flane == blk, mean, km)
        kmean.append(km)
    gate = [jnp.dot(qbd[u], kmean[u], precision=lax.Precision.HIGHEST,
                    preferred_element_type=f32) for u in seqs]

    scores, s_new, m = [], [], []
    for u in seqs:
        sel = _select_blocks(gate[u], N_PAST_BLOCKS, N_PAST_BLOCKS, 1)
        blk_bias = jnp.where(sel, 0.0, NEG)
        sn = jnp.where(tk <= tq, raw_new[u] - slope * (tq - tk).astype(f32), NEG)
        mu = jnp.max(sn, axis=1, keepdims=True)
        su = []
        for pg in range(N_PAGES):
            dist = (tq + (PAST_LEN - pg * PAGE_SIZE) - kcol).astype(f32)
            blk = pg // PAGES_PER_BLOCK
            s = raw[u][pg] - slope * dist + blk_bias[:, blk:blk + 1]
            mu = jnp.maximum(mu, jnp.max(s, axis=1, keepdims=True))
            su.append(s)
        scores.append(su)
        s_new.append(sn)
        m.append(mu)

    for u in seqs:
        p_new = jnp.exp(s_new[u] - m[u])
        l = jnp.sum(p_new, axis=1, keepdims=True)
        acc = jnp.dot(p_new.astype(bf16), vn_ref[u].astype(bf16), preferred_element_type=f32)
        for pg in range(N_PAGES):
            pe = jnp.exp(scores[u][pg] - m[u])
            l = l + jnp.sum(pe, axis=1, keepdims=True)
            acc = acc + lax.dot_general(pe.astype(bf16), v_refs[u][pg][...].astype(bf16), _NT,
                                        preferred_element_type=f32)
        out = jnp.where(head_lanes, acc / l, 0.0)
        o_ref[u] = jnp.sum(out.reshape(DEC_SEQ, B_HEADS, B_WIDTH), axis=1)


def _moba_sample(page_table, q, k_new, v_new, cache_k, cache_v, layer):
    seq_spec = lambda r: pl.BlockSpec((SEQS_PER_STEP, r, B_WIDTH), lambda b, pt: (b, 0, 0))

    def page_spec(u, pg):
        return pl.BlockSpec(
            (None, None, B_WIDTH, PAGE_SIZE),
            lambda b, pt: (layer, pt[(b * SEQS_PER_STEP + u) * N_PAGES + pg], 0, 0))

    pages, operands = [], []
    for u in range(SEQS_PER_STEP):
        for cache in (cache_k, cache_v):
            pages += [page_spec(u, pg) for pg in range(N_PAGES)]
            operands += [cache] * N_PAGES
    return pl.pallas_call(
        _moba_sample_kernel,
        grid_spec=pltpu.PrefetchScalarGridSpec(
            num_scalar_prefetch=1,
            grid=(DEC_BATCH // SEQS_PER_STEP,),
            in_specs=[seq_spec(DEC_SEQ * B_HEADS), seq_spec(NEW_PAD), seq_spec(NEW_PAD)] + pages,
            out_specs=seq_spec(DEC_SEQ)),
        out_shape=jax.ShapeDtypeStruct((DEC_BATCH, DEC_SEQ, B_WIDTH), f32),
        compiler_params=_params(("parallel",)),
        name="moba_sample",
    )(page_table, q, k_new, v_new, *operands)


def _even_out_kernel(x_ref, ya_ref, ybt_ref, ybs_ref, w_ref, o_ref):
    i = pl.program_id(0)
    is_prompt = i < N_PROMPT_TILES
    base = x_ref[...] + jnp.dot(ya_ref[...], w_ref[:A_WIDTH, :], preferred_element_type=f32)

    @pl.when(is_prompt)
    def _():
        o_ref[...] = base + lax.dot_general(ybt_ref[0], w_ref[A_WIDTH:, :], _TN,
                                            preferred_element_type=f32)

    @pl.when(jnp.logical_not(is_prompt))
    def _():
        o_ref[...] = base + jnp.dot(ybs_ref[...], w_ref[A_WIDTH:, :], preferred_element_type=f32)


def _even_out(x, ya, yb_t, yb_s, w_out):
    return pl.pallas_call(
        _even_out_kernel,
        grid=(N_TILES,),
        in_specs=[_tile_spec(D_MODEL), _tile_spec(A_WIDTH), _feature_major_spec(),
                  _const_spec((TM, B_WIDTH)), _const_spec((D_MODEL, D_MODEL))],
        out_specs=_tile_spec(D_MODEL),
        out_shape=jax.ShapeDtypeStruct((N_TOK, D_MODEL), f32),
        compiler_params=_params(("arbitrary",)),
        name="even_out",
    )(x, ya, yb_t, yb_s, w_out)


def _odd_kernel(x_ref, g_ref, wi_ref, cw_ref, wo_ref, st_ref,
                o_ref, cp_ref, cs_ref, gated_ref, ext_ref):
    i = pl.program_id(0)
    is_prompt = i < N_PROMPT_TILES
    h = _rms(x_ref[...], g_ref[...]).astype(bf16)

    @pl.when(i % TILES_PER_SEQ == 0)
    def _():
        ext_ref[CONV_FRONT - SUBLANES:CONV_FRONT, :] = jnp.zeros((SUBLANES, C_WIDTH), f32)

    @pl.when(jnp.logical_not(is_prompt))
    def _():
        ext_ref[:CONV_FRONT, :] = st_ref[...]

    def project(part, c):
        lo = part * C_WIDTH + c * CONV_CHUNK
        return jnp.dot(h, wi_ref[:, lo:lo + CONV_CHUNK], preferred_element_type=f32)

    for c in range(C_WIDTH // CONV_CHUNK):
        cs = slice(c * CONV_CHUNK, (c + 1) * CONV_CHUNK)
        cx = project(2, c) * project(0, c)
        ext_ref[CONV_FRONT:, cs] = cx
        back1 = jnp.where(is_prompt, ext_ref[CONV_FRONT - 1:CONV_FRONT - 1 + TM, cs],
                          ext_ref[CONV_FRONT - DEC_BATCH:CONV_FRONT - DEC_BATCH + TM, cs])
        back2 = jnp.where(is_prompt, ext_ref[CONV_FRONT - 2:CONV_FRONT - 2 + TM, cs],
                          ext_ref[:TM, cs])
        conv = cw_ref[0:1, cs] * back2 + cw_ref[1:2, cs] * back1 + cw_ref[2:3, cs] * cx
        gated_ref[:, cs] = (project(1, c) * conv).astype(bf16)
    o_ref[...] = x_ref[...] + jnp.dot(gated_ref[...], wo_ref[...], preferred_element_type=f32)

    @pl.when(is_prompt)
    def _():
        last = ext_ref[CONV_FRONT + TM - SUBLANES:, :]
        ext_ref[CONV_FRONT - SUBLANES:CONV_FRONT, :] = last
        cp_ref[0] = last

    @pl.when(i == N_TILES - 1)
    def _():
        cs_ref[...] = ext_ref[TM:, :]


def _odd(x, g, w_in, conv_w, w_out, state):
    return pl.pallas_call(
        _odd_kernel,
        grid=(N_TILES,),
        in_specs=[_tile_spec(D_MODEL), _const_spec((1, D_MODEL)),
                  _const_spec((D_MODEL, 3 * C_WIDTH)), _const_spec((CONV_W, C_WIDTH)),
                  _const_spec((C_WIDTH, D_MODEL)), _const_spec((CONV_FRONT, C_WIDTH))],
        out_specs=[_tile_spec(D_MODEL),
                   pl.BlockSpec((1, SUBLANES, C_WIDTH),
                                lambda i: (jnp.minimum(i // TILES_PER_SEQ, BATCH - 1), 0, 0)),
                   pl.BlockSpec((CONV_FRONT, C_WIDTH), lambda i: (0, 0))],
        out_shape=[jax.ShapeDtypeStruct((N_TOK, D_MODEL), f32),
                   jax.ShapeDtypeStruct((BATCH, SUBLANES, C_WIDTH), f32),
                   jax.ShapeDtypeStruct((CONV_FRONT, C_WIDTH), f32)],
        scratch_shapes=[pltpu.VMEM((TM, C_WIDTH), bf16),
                        pltpu.VMEM((CONV_FRONT + TM, C_WIDTH), f32)],
        compiler_params=_params(("arbitrary",)),
        name="odd_mixer",
    )(x, g, w_in, conv_w, w_out, state)


def _step_major(a):
    return jnp.swapaxes(a, 0, 1).reshape(a.shape[0] * a.shape[1], a.shape[2])


def _batch_major(a, steps=DEC_SEQ):
    return jnp.swapaxes(a.reshape(steps, DEC_BATCH, a.shape[-1]), 0, 1)


def kernel(x_prompt, x_sample, cache_k, cache_v, state_conv, page_table, ffn1_norm, ffn1_gate, ffn1_up, ffn1_down, mix_norm, ffn2_norm, ffn2_gate, ffn2_up, ffn2_down, even_w_in, even_w_out, a_v_gain, a_w_s, a_b_s, odd_w_in, odd_conv_w, odd_w_out, final_norm):
    n_phys = cache_k.shape[1]
    pages = lambda c: jnp.transpose(c, (0, 1, 3, 4, 2)).reshape(N_EVEN, n_phys, B_WIDTH, PAGE_SIZE)
    ck, cv = pages(cache_k), pages(cache_v)
    pt = page_table.reshape(-1).astype(jnp.int32)
    row = lambda a: a.reshape(1, -1)
    fg = row(final_norm)
    xs = (x_prompt.reshape(N_PROMPT, D_MODEL), _step_major(x_sample))

    k_s, v_s, conv_p, conv_s, av_s = [], [], [], [], []
    kv_stacks = ()
    for i in range(DEPTH):
        j = i // 2
        (x,) = _ffn(xs, row(ffn1_norm[i]), ffn1_gate, ffn1_up, ffn1_down, i, fg, first=(i == 0))
        if i % 2 == 0:
            reps = TM // A_CHUNK
            w4 = a_w_s[j][:, :DEC_SEQ, :DEC_SEQ]
            ws_tiled = jnp.stack([
                jnp.tile(a_w_s[j], (1, reps, reps)),
                jnp.repeat(jnp.repeat(w4, DEC_BATCH, axis=1), DEC_BATCH, axis=2)])
            bs_tiled = jnp.stack([
                jnp.tile(a_b_s[j], (1, reps)),
                jnp.repeat(a_b_s[j][:, :DEC_SEQ], DEC_BATCH, axis=1)])[..., None]
            w_in = even_w_in[j].astype(bf16)
            w_row = jnp.concatenate([w_in[:, :2 * A_WIDTH],
                                     w_in[:, 2 * A_WIDTH + B_WIDTH:2 * A_WIDTH + 2 * B_WIDTH]], axis=1)
            w_t = w_in[:, 2 * A_WIDTH:].T
            ya, va, q_t, k_stack, v_stack, vtb, kb, km, st = _even_in(
                x, row(mix_norm[i]), w_row, w_t, row(a_v_gain[j]), ws_tiled, bs_tiled, j, kv_stacks)
            kv_stacks = (k_stack, v_stack)
            kmean = km.reshape(N_TOK // MOBA_BLOCK, B_WIDTH)[:BATCH * N_BLOCKS]
            yb_t = _moba_prompt(q_t, kb, vtb, kmean.reshape(BATCH, N_BLOCKS, B_WIDTH))
            per_seq = jnp.transpose(st, (0, 3, 1, 2))
            padn = lambda a: jnp.pad(a, ((0, 0), (0, NEW_PAD - DEC_SEQ), (0, 0)))
            q_rep = jnp.repeat(per_seq[0], B_HEADS, axis=1)
            yb_s = _moba_sample(pt, q_rep, padn(per_seq[1]), padn(per_seq[2]), ck, cv, j)
            mix = (ya, yb_t, _step_major(yb_s).astype(bf16), even_w_out[j].astype(bf16))
            k_s.append(st[1].reshape(DEC_SEQ, B_HEADS, B_HEAD_DIM, DEC_BATCH))
            v_s.append(st[2].reshape(DEC_SEQ, B_HEADS, B_HEAD_DIM, DEC_BATCH))
            av_s.append(_batch_major(va))
        else:
            state = jnp.swapaxes(state_conv[j], 0, 1).reshape(CONV_FRONT, C_WIDTH)
            x, cp, cs = _odd(x, row(mix_norm[i]), odd_w_in[j].astype(bf16), odd_conv_w[j],
                             odd_w_out[j].astype(bf16), state)
            conv_p.append(cp[:, SUBLANES - (CONV_W - 1):])
            conv_s.append(_batch_major(cs, CONV_W - 1))
            mix = ()
        xs = _ffn((x,), row(ffn2_norm[i]), ffn2_gate, ffn2_up, ffn2_down, i, fg,
                  final=(i == DEPTH - 1), mix=mix)

    y_prompt = xs[0].reshape(BATCH, SEQ, D_MODEL)
    y_sample = _batch_major(xs[1])
    to_rows_p = lambda a: jnp.transpose(
        a.reshape(N_EVEN, BATCH, B_HEADS, B_HEAD_DIM, SEQ), (0, 1, 4, 2, 3))
    to_rows_s = lambda ts: jnp.transpose(jnp.stack(ts), (0, 4, 1, 2, 3))
    return (y_prompt, y_sample, to_rows_p(kv_stacks[0]), to_rows_p(kv_stacks[1]),
            jnp.stack(conv_p),
            to_rows_s(k_s), to_rows_s(v_s), jnp.stack(conv_s), jnp.stack(av_s))
```

```python
import functools
import struct

import jax
import jax.numpy as jnp
from jax import lax
from jax.experimental import pallas as pl
from jax.experimental.pallas import tpu as pltpu

D_MODEL = 1024
BATCH = 4
SEQ = 4096
DEPTH = 4
DEC_BATCH = 128
DEC_SEQ = 4
PAST_LEN = 2048
PAGE_SIZE = 128
N_EVEN = (DEPTH + 1) // 2
N_ODD = DEPTH // 2
D_FF = 2816
RMS_EPS = 1e-6
A_WIDTH = D_MODEL // 2
A_CHUNK = 128
A_GROUPS = 4
A_GROUP_DIM = A_WIDTH // A_GROUPS
B_HEAD_DIM = 64
B_WIDTH = D_MODEL - A_WIDTH
B_HEADS = B_WIDTH // B_HEAD_DIM
MOBA_BLOCK = 256
MOBA_TOPK = 3
C_WIDTH = D_MODEL
CONV_W = 3

N_PROMPT = BATCH * SEQ
N_SAMPLE = DEC_BATCH * DEC_SEQ
N_TOK = N_PROMPT + N_SAMPLE
TM = 512
N_TILES = N_TOK // TM
N_PROMPT_TILES = N_PROMPT // TM
TILES_PER_SEQ = SEQ // TM
FF_CHUNK = 256
N_FF_CHUNKS = D_FF // FF_CHUNK
N_PAGES = PAST_LEN // PAGE_SIZE
N_BLOCKS = SEQ // MOBA_BLOCK
N_PAST_BLOCKS = PAST_LEN // MOBA_BLOCK
PAGES_PER_BLOCK = MOBA_BLOCK // PAGE_SIZE
BLOCKS_PER_TILE = TM // MOBA_BLOCK
NEW_PAD = 16
HEAD_PAIR = 2 * B_HEAD_DIM
N_HEAD_PAIRS = B_HEADS // 2
SLOPES = tuple(2.0 ** (-8.0 * (h + 1) / B_HEADS) for h in range(B_HEADS))
NEG = -1e30
SUBLANES = 8
LANES = 128
CONV_FRONT = 2 * DEC_BATCH
CONV_CHUNK = 256
VMEM_LIMIT = 56 * 1024 * 1024

assert N_SAMPLE == TM and N_TOK % TM == 0 and D_FF % FF_CHUNK == 0
assert DEC_BATCH == LANES and CONV_W - 1 == 2 and DEC_SEQ >= CONV_W - 1

_NT = (((1,), (1,)), ((), ()))
_TN = (((0,), (0,)), ((), ()))
bf16 = jnp.bfloat16
f32 = jnp.float32


def _log2(n):
    assert n & (n - 1) == 0
    return n.bit_length() - 1


def _rms(x, g):
    ms = jnp.mean(x * x, axis=-1, keepdims=True)
    return x * lax.rsqrt(ms + RMS_EPS) * g


def _params(sem):
    return pltpu.CompilerParams(dimension_semantics=sem, vmem_limit_bytes=VMEM_LIMIT)


def _const_spec(shape):
    nd = len(shape)
    return pl.BlockSpec(shape, lambda *_: (0,) * nd, pipeline_mode=pl.Buffered(1))


def _tile_spec(width):
    return pl.BlockSpec((TM, width), lambda i: (i, 0))


def _prompt_tile_spec(width):
    return pl.BlockSpec((TM, width), lambda i: (jnp.minimum(i, N_PROMPT_TILES - 1), 0))


def _feature_major_index(i):
    j = jnp.minimum(i, N_PROMPT_TILES - 1)
    return (j // TILES_PER_SEQ, 0, j % TILES_PER_SEQ)


def _feature_major_spec():
    return pl.BlockSpec((1, B_WIDTH, TM), _feature_major_index)


def _layered_feature_major_spec(layer, n_slots):
    assert layer % n_slots == 0
    return pl.BlockSpec((n_slots, 1, B_WIDTH, TM),
                        lambda i: (layer // n_slots,) + _feature_major_index(i))


def _ffn_kernel(*refs, first, final, mixed, layer):
    n_x = (2 if first else 1) + (4 if mixed else 0)
    x_refs, (g_ref, wg_hbm, wu_hbm, wd_hbm, fg_ref) = refs[:n_x], refs[n_x:n_x + 5]
    n_out = 2 if final else 1
    o_refs = refs[n_x + 5:n_x + 5 + n_out]
    scratch = list(refs[n_x + 5 + n_out:])
    xin_ref = scratch.pop(0) if mixed else None
    acc_ref, wg_ref, wu_ref, wd_ref, sem = scratch
    i = pl.program_id(0)
    is_prompt = i < N_PROMPT_TILES

    def chunk_copies(c):
        sl = slice(c * FF_CHUNK, (c + 1) * FF_CHUNK)
        return (pltpu.make_async_copy(wg_hbm.at[layer, :, sl], wg_ref.at[:, sl], sem.at[0, c]),
                pltpu.make_async_copy(wu_hbm.at[layer, :, sl], wu_ref.at[:, sl], sem.at[1, c]),
                pltpu.make_async_copy(wd_hbm.at[layer, sl, :], wd_ref.at[sl, :], sem.at[2, c]))

    @pl.when(i == 0)
    def _():
        for c in range(N_FF_CHUNKS):
            for copy in chunk_copies(c):
                copy.start()

    if first:
        x = jnp.where(is_prompt, x_refs[0][...], x_refs[1][...])
    elif mixed:
        x_ref, ya_ref, ybt_ref, ybs_ref, wo_ref = x_refs
        base = x_ref[...] + jnp.dot(ya_ref[...], wo_ref[:A_WIDTH, :], preferred_element_type=f32)

        @pl.when(is_prompt)
        def _():
            xin_ref[...] = base + lax.dot_general(ybt_ref[0], wo_ref[A_WIDTH:, :], _TN,
                                                  preferred_element_type=f32)

        @pl.when(jnp.logical_not(is_prompt))
        def _():
            xin_ref[...] = base + jnp.dot(ybs_ref[...], wo_ref[A_WIDTH:, :],
                                          preferred_element_type=f32)

        x = xin_ref[...]
    else:
        x = x_refs[0][...]

    def body(wait_for_weights):
        h = _rms(x, g_ref[...]).astype(bf16)
        for c in range(N_FF_CHUNKS):
            sl = slice(c * FF_CHUNK, (c + 1) * FF_CHUNK)
            if wait_for_weights:
                for copy in chunk_copies(c):
                    copy.wait()
            gate = jnp.dot(h, wg_ref[:, sl].astype(bf16), preferred_element_type=f32)
            up = jnp.dot(h, wu_ref[:, sl].astype(bf16), preferred_element_type=f32)
            act = (gate / (1.0 + jnp.exp(-gate)) * up).astype(bf16)
            down = jnp.dot(act, wd_ref[sl, :].astype(bf16), preferred_element_type=f32)
            if c == 0:
                acc_ref[...] = down
            else:
                acc_ref[...] += down
        y = x + 0.5 * acc_ref[...]
        if final:
            y = _rms(y, fg_ref[...])

            @pl.when(is_prompt)
            def _():
                o_refs[0][...] = y

            @pl.when(jnp.logical_not(is_prompt))
            def _():
                o_refs[1][...] = y
        else:
            o_refs[0][...] = y

    @pl.when(i == 0)
    def _():
        body(True)

    @pl.when(i != 0)
    def _():
        body(False)


def _ffn(xs, g, wg, wu, wd, layer, fg, first=False, final=False, mix=()):
    in_hbm = pl.BlockSpec(memory_space=pl.ANY)
    sample_spec = pl.BlockSpec((TM, D_MODEL), lambda i: (0, 0))
    x_specs = [_prompt_tile_spec(D_MODEL), sample_spec] if first else [_tile_spec(D_MODEL)]
    scratch = [pltpu.VMEM((TM, D_MODEL), f32),
               pltpu.VMEM((D_MODEL, D_FF), f32), pltpu.VMEM((D_MODEL, D_FF), f32),
               pltpu.VMEM((D_FF, D_MODEL), f32),
               pltpu.SemaphoreType.DMA((3, N_FF_CHUNKS))]
    if mix:
        x_specs += [_tile_spec(A_WIDTH), _feature_major_spec(),
                    _const_spec((TM, B_WIDTH)), _const_spec((D_MODEL, D_MODEL))]
        scratch = [pltpu.VMEM((TM, D_MODEL), f32)] + scratch
    if final:
        out_specs = [_prompt_tile_spec(D_MODEL), sample_spec]
        out_shape = [jax.ShapeDtypeStruct((N_PROMPT, D_MODEL), f32),
                     jax.ShapeDtypeStruct((N_SAMPLE, D_MODEL), f32)]
    else:
        out_specs = [_tile_spec(D_MODEL)]
        out_shape = [jax.ShapeDtypeStruct((N_TOK, D_MODEL), f32)]
    return pl.pallas_call(
        functools.partial(_ffn_kernel, first=first, final=final, mixed=bool(mix), layer=layer),
        grid=(N_TILES,),
        in_specs=x_specs + [_const_spec((1, D_MODEL)), in_hbm, in_hbm, in_hbm,
                            _const_spec((1, D_MODEL))],
        out_specs=out_specs,
        out_shape=out_shape,
        scratch_shapes=scratch,
        compiler_params=_params(("arbitrary",)),
        name="ffn",
    )(*xs, *mix, g, wg, wu, wd, fg)


ROW_COLS = 2 * A_WIDTH + B_WIDTH
T_ROWS = 3 * B_WIDTH


def _even_in_kernel(*refs, n_prev):
    (x_ref, g_ref, w_ref, wt_ref, gain_ref, ws_ref, bs_ref) = refs[:7]
    (ya_ref, vs_ref, qt_ref, kt_ref, vt_ref, vtb_ref, kb_ref, km_ref, st_ref,
     z_ref, zt_ref) = refs[7 + n_prev:]
    i = pl.program_id(0)
    is_prompt = i < N_PROMPT_TILES
    h = _rms(x_ref[...], g_ref[...]).astype(bf16)
    z_ref[...] = jnp.dot(h, w_ref[...], preferred_element_type=f32)
    zt_ref[...] = lax.dot_general(wt_ref[...], h, _NT, preferred_element_type=f32)

    def gelu(a):
        return a * (0.5 * (1.0 + jnp.tanh(0.7978845608028654 * (a + 0.044715 * (a * a * a)))))

    row = lax.broadcasted_iota(jnp.int32, (TM, TM), 0)
    col = lax.broadcasted_iota(jnp.int32, (TM, TM), 1)
    chunk = lambda r: jnp.where(is_prompt, r >> _log2(A_CHUNK), r & (DEC_BATCH - 1))
    keep = (chunk(row) == chunk(col)) & (col <= row)
    for g in range(A_GROUPS):
        gs = slice(g * A_GROUP_DIM, (g + 1) * A_GROUP_DIM)
        u = gelu(z_ref[:, gs])
        v = gelu(z_ref[:, A_WIDTH + g * A_GROUP_DIM:A_WIDTH + (g + 1) * A_GROUP_DIM])
        vn = _rms(v, gain_ref[:, gs])
        w = jnp.where(keep, ws_ref[0, g], 0.0).astype(bf16)
        mixed = jnp.dot(w, vn.astype(bf16), preferred_element_type=f32) + bs_ref[0, g]
        ya_ref[:, gs] = (u * mixed).astype(bf16)

        @pl.when(i == N_TILES - 1)
        def _():
            vs_ref[:, gs] = vn

    k = z_ref[:, 2 * A_WIDTH:]
    kb_ref[...] = k.astype(bf16)
    for blk in range(BLOCKS_PER_TILE):
        ks = jnp.sum(k[blk * MOBA_BLOCK:(blk + 1) * MOBA_BLOCK], axis=0, keepdims=True)
        km_ref[0, blk:blk + 1, :] = ks * (1.0 / MOBA_BLOCK)

    @pl.when(is_prompt)
    def _():
        qt_ref[0] = zt_ref[:B_WIDTH, :]
        kt = zt_ref[B_WIDTH:2 * B_WIDTH, :]
        vt = zt_ref[2 * B_WIDTH:, :]
        for slot in range(kt_ref.shape[0]):
            kt_ref[slot, 0] = kt
            vt_ref[slot, 0] = vt
        vtb_ref[0] = vt.astype(bf16)

    @pl.when(i == N_TILES - 1)
    def _():
        for part in range(3):
            for t in range(DEC_SEQ):
                st_ref[part, t] = zt_ref[part * B_WIDTH:(part + 1) * B_WIDTH,
                                         t * DEC_BATCH:(t + 1) * DEC_BATCH]


def _even_in(x, g, w_row, w_t, gain, ws_tiled, bs_tiled, layer, kv_stacks):
    kind = lambda i: (i // N_PROMPT_TILES, 0, 0, 0)
    fm = lambda dt: jax.ShapeDtypeStruct((BATCH, B_WIDTH, SEQ), dt)
    stack = jax.ShapeDtypeStruct((N_EVEN, BATCH, B_WIDTH, SEQ), f32)
    n_in = 7
    stack_spec = _layered_feature_major_spec(layer, 1 if kv_stacks else N_EVEN)
    return pl.pallas_call(
        functools.partial(_even_in_kernel, n_prev=len(kv_stacks)),
        grid=(N_TILES,),
        in_specs=[_tile_spec(D_MODEL), _const_spec((1, D_MODEL)),
                  _const_spec((D_MODEL, ROW_COLS)), _const_spec((T_ROWS, D_MODEL)),
                  _const_spec((1, A_WIDTH)),
                  pl.BlockSpec((1, A_GROUPS, TM, TM), kind),
                  pl.BlockSpec((1, A_GROUPS, TM, 1), kind)]
        + [pl.BlockSpec(memory_space=pl.ANY)] * len(kv_stacks),
        out_specs=[_tile_spec(A_WIDTH),
                   pl.BlockSpec((TM, A_WIDTH), lambda i: (0, 0)),
                   _feature_major_spec(), stack_spec, stack_spec,
                   _feature_major_spec(),
                   _tile_spec(B_WIDTH),
                   pl.BlockSpec((1, BLOCKS_PER_TILE, B_WIDTH), lambda i: (i, 0, 0)),
                   pl.BlockSpec((3, DEC_SEQ, B_WIDTH, DEC_BATCH), lambda i: (0, 0, 0, 0))],
        out_shape=[jax.ShapeDtypeStruct((N_TOK, A_WIDTH), bf16),
                   jax.ShapeDtypeStruct((N_SAMPLE, A_WIDTH), f32),
                   fm(f32), stack, stack, fm(bf16),
                   jax.ShapeDtypeStruct((N_TOK, B_WIDTH), bf16),
                   jax.ShapeDtypeStruct((N_TILES, BLOCKS_PER_TILE, B_WIDTH), f32),
                   jax.ShapeDtypeStruct((3, DEC_SEQ, B_WIDTH, DEC_BATCH), f32)],
        scratch_shapes=[pltpu.VMEM((TM, ROW_COLS), f32), pltpu.VMEM((T_ROWS, TM), f32)],
        input_output_aliases={n_in + s: 3 + s for s in range(len(kv_stacks))},
        compiler_params=_params(("arbitrary",)),
        name="even_in",
    )(x, g, w_row, w_t, gain, ws_tiled, bs_tiled, *kv_stacks)


def _select_blocks(gate, n_valid, n_past, axis):
    blk = lax.broadcasted_iota(jnp.int32, gate.shape, axis)
    valid = blk < n_valid
    g = jnp.where(valid, gate, -jnp.inf)
    sel = jnp.zeros(gate.shape, jnp.bool_)
    for _ in range(min(MOBA_TOPK, n_past)):
        mx = jnp.max(g, axis=axis, keepdims=True)
        idx = jnp.min(jnp.where(g == mx, blk, n_past), axis=axis, keepdims=True)
        hit = blk == idx
        sel = sel | (hit & valid)
        g = jnp.where(hit, -jnp.inf, g)
    return sel


PAIR_COLS = 2 * MOBA_BLOCK
ALL_COLS = B_HEADS * MOBA_BLOCK


LOG2E = 1.4426950408889634


def _round_bf16(x):
    bits = struct.unpack("<I", struct.pack("<f", x))[0]
    bits = (bits + 0x7FFF + ((bits >> 16) & 1)) & 0xFFFF0000
    return struct.unpack("<f", struct.pack("<I", bits))[0]


def _bf16_terms(x, n):
    terms = []
    for _ in range(n):
        t = _round_bf16(x)
        terms.append(t)
        x -= t
    return terms


SLOPE_TERMS = 4
SUM_ROWS = 16
KEY_UNROLL = 2


def _moba_prompt_kernel(q_ref, k_ref, v_ref, km_ref, o_ref,
                        kpos_ref, qbd_ref, bias_ref, s0_ref, s1_ref, m_ref, acc_ref):
    b = pl.program_id(0)
    i = pl.program_id(1)

    @pl.when((b == 0) & (i == 0))
    def _():
        key = lax.broadcasted_iota(jnp.int32, (MOBA_BLOCK, HEAD_PAIR), 0)
        col = lax.broadcasted_iota(jnp.int32, (MOBA_BLOCK, HEAD_PAIR), 1)
        kpos_ref[...] = jnp.where(col < SLOPE_TERMS, key, 0).astype(bf16)
        r = lax.broadcasted_iota(jnp.int32, (HEAD_PAIR, PAIR_COLS), 0)
        e_of_col = lax.broadcasted_iota(jnp.int32, (HEAD_PAIR, PAIR_COLS), 1) >> _log2(MOBA_BLOCK)
        for p in range(N_HEAD_PAIRS):
            slope_rows = jnp.zeros((HEAD_PAIR, PAIR_COLS), f32)
            for e in range(2):
                for t, term in enumerate(_bf16_terms(SLOPES[2 * p + e] * LOG2E, SLOPE_TERMS)):
                    slope_rows = jnp.where((r == t) & (e_of_col == e), term, slope_rows)
            qbd_ref[p, HEAD_PAIR:, :] = slope_rows.astype(bf16)

    feat = lax.broadcasted_iota(jnp.int32, (HEAD_PAIR, PAIR_COLS), 0)
    qcol = lax.broadcasted_iota(jnp.int32, (HEAD_PAIR, PAIR_COLS), 1)
    mine = (feat >> _log2(B_HEAD_DIM)) == (qcol >> _log2(MOBA_BLOCK))
    blk = lax.broadcasted_iota(jnp.int32, (N_BLOCKS, PAIR_COLS), 0)
    bcol = lax.broadcasted_iota(jnp.int32, (N_BLOCKS, PAIR_COLS), 1)
    blocks_back = (i - blk).astype(f32) * float(MOBA_BLOCK)
    for p in range(N_HEAD_PAIRS):
        ls = slice(p * HEAD_PAIR, (p + 1) * HEAD_PAIR)
        q2 = q_ref[0, ls, :]
        qmf = jnp.where(mine, jnp.concatenate([q2, q2], axis=1), 0.0)
        qbd_ref[p, :HEAD_PAIR, :] = (qmf * (LOG2E * B_HEAD_DIM ** -0.5)).astype(bf16)
        gate = jnp.dot(km_ref[0, :, ls], qmf, precision=lax.Precision.HIGHEST,
                       preferred_element_type=f32)
        sel = _select_blocks(gate, i, N_BLOCKS - 1, 0)
        slope = jnp.where(bcol < MOBA_BLOCK, SLOPES[2 * p] * LOG2E, SLOPES[2 * p + 1] * LOG2E)
        bias = jnp.where(sel, -slope * blocks_back, NEG)
        bias_ref[:, p * PAIR_COLS:(p + 1) * PAIR_COLS] = jnp.where(blk == i, 0.0, bias)
    m_ref[...] = jnp.full((1, ALL_COLS), NEG, f32)
    acc_ref[...] = jnp.zeros((B_HEADS, B_HEAD_DIM + SUM_ROWS, MOBA_BLOCK), f32)

    def scores(n, s_ref, own):
        start = pl.multiple_of(n * MOBA_BLOCK, MOBA_BLOCK)
        for p in range(N_HEAD_PAIRS):
            kn = k_ref[pl.ds(start, MOBA_BLOCK), p * HEAD_PAIR:(p + 1) * HEAD_PAIR]
            kn = jnp.concatenate([kn, kpos_ref[...]], axis=1)
            s = jnp.dot(kn, qbd_ref[p], preferred_element_type=f32)
            if own:
                key = lax.broadcasted_iota(jnp.int32, (MOBA_BLOCK, PAIR_COLS), 0)
                query = lax.broadcasted_iota(jnp.int32, (MOBA_BLOCK, PAIR_COLS), 1) & (MOBA_BLOCK - 1)
                s = jnp.where(key <= query, s, NEG)
            s_ref[p] = s

    def absorb(n, s_ref):
        start = pl.multiple_of(n * MOBA_BLOCK, MOBA_BLOCK)
        for p in range(N_HEAD_PAIRS):
            cs = slice(p * PAIR_COLS, (p + 1) * PAIR_COLS)
            s = s_ref[p]
            brow = bias_ref[pl.ds(n, 1), cs]
            m_old = m_ref[:, cs]
            m_new = jnp.maximum(m_old, jnp.max(s, axis=0, keepdims=True) + brow)
            alpha = jnp.exp2(m_old - m_new)
            peb = jnp.exp2(s - (m_new - brow)).astype(bf16)
            m_ref[:, cs] = m_new
            for e in range(2):
                h = 2 * p + e
                es = slice(e * MOBA_BLOCK, (e + 1) * MOBA_BLOCK)
                vn = v_ref[0, h * B_HEAD_DIM:(h + 1) * B_HEAD_DIM, pl.ds(start, MOBA_BLOCK)]
                vn = jnp.concatenate([vn, jnp.ones((SUM_ROWS, MOBA_BLOCK), bf16)], axis=0)
                acc_ref[h] = alpha[:, es] * acc_ref[h] + jnp.dot(
                    vn, peb[:, es], preferred_element_type=f32)

    slots = (s0_ref, s1_ref)
    scores(i, s0_ref, True)

    def steps(first, count):
        for u in range(count):
            before = jnp.where(first == 0, i, first - 1) if u == 0 else first + u - 1
            scores(first + u, slots[(u + 1) % 2], False)
            absorb(before, slots[u % 2])

    def trip(jj, carry):
        steps(KEY_UNROLL * jj, KEY_UNROLL)
        return carry

    lax.fori_loop(0, i // KEY_UNROLL, trip, 0)
    done = (i // KEY_UNROLL) * KEY_UNROLL
    for rest in range(KEY_UNROLL):
        @pl.when(i - done == rest)
        def _():
            steps(done, rest)
            absorb(jnp.maximum(i - 1, 0), slots[rest % 2])

    for h in range(B_HEADS):
        o_ref[0, h * B_HEAD_DIM:(h + 1) * B_HEAD_DIM, :] = (
            acc_ref[h, :B_HEAD_DIM, :] / acc_ref[h, B_HEAD_DIM:B_HEAD_DIM + 1, :]).astype(bf16)


def _moba_prompt(q_t, kb, vtb, kmean):
    qspec = pl.BlockSpec((1, B_WIDTH, MOBA_BLOCK), lambda b, i: (b, 0, i))
    return pl.pallas_call(
        _moba_prompt_kernel,
        grid=(BATCH, N_BLOCKS),
        in_specs=[qspec,
                  pl.BlockSpec((SEQ, B_WIDTH), lambda b, i: (b, 0)),
                  pl.BlockSpec((1, B_WIDTH, SEQ), lambda b, i: (b, 0, 0)),
                  pl.BlockSpec((1, N_BLOCKS, B_WIDTH), lambda b, i: (b, 0, 0))],
        out_specs=qspec,
        out_shape=jax.ShapeDtypeStruct((BATCH, B_WIDTH, SEQ), bf16),
        scratch_shapes=[pltpu.VMEM((MOBA_BLOCK, HEAD_PAIR), bf16),
                        pltpu.VMEM((N_HEAD_PAIRS, 2 * HEAD_PAIR, PAIR_COLS), bf16),
                        pltpu.VMEM((N_BLOCKS, ALL_COLS), f32),
                        pltpu.VMEM((N_HEAD_PAIRS, MOBA_BLOCK, PAIR_COLS), f32),
                        pltpu.VMEM((N_HEAD_PAIRS, MOBA_BLOCK, PAIR_COLS), f32),
                        pltpu.VMEM((1, ALL_COLS), f32),
                        pltpu.VMEM((B_HEADS, B_HEAD_DIM + SUM_ROWS, MOBA_BLOCK), f32)],
        compiler_params=_params(("arbitrary", "arbitrary")),
        name="moba_prompt",
    )(q_t, kb, vtb, kmean)


SEQS_PER_STEP = 2


def _moba_sample_kernel(pt_ref, q_ref, kn_ref, vn_ref, *refs):
    seqs = range(SEQS_PER_STEP)
    k_refs = [refs[2 * u * N_PAGES:(2 * u + 1) * N_PAGES] for u in seqs]
    v_refs = [refs[(2 * u + 1) * N_PAGES:(2 * u + 2) * N_PAGES] for u in seqs]
    o_ref = refs[2 * SEQS_PER_STEP * N_PAGES]
    rows = DEC_SEQ * B_HEADS
    row = lax.broadcasted_iota(jnp.int32, (rows, B_WIDTH), 0)
    lane = lax.broadcasted_iota(jnp.int32, (rows, B_WIDTH), 1)
    head_lanes = (lane >> _log2(B_HEAD_DIM)) == (row & (B_HEADS - 1))
    r1 = lax.broadcasted_iota(jnp.int32, (rows, 1), 0)
    hrow = r1 & (B_HEADS - 1)
    slope = jnp.zeros((rows, 1), f32)
    for h in range(B_HEADS):
        slope = jnp.where(hrow == h, SLOPES[h], slope)
    tq = r1 >> _log2(B_HEADS)
    kcol = lax.broadcasted_iota(jnp.int32, (rows, PAGE_SIZE), 1)
    tk = lax.broadcasted_iota(jnp.int32, (rows, NEW_PAD), 1)
    flane = lax.broadcasted_iota(jnp.int32, (B_WIDTH, PAGE_SIZE), 1)

    qbd, raw, raw_new = [], [], []
    for u in seqs:
        qf = jnp.where(head_lanes, q_ref[u], 0.0)
        qb = (qf * (B_HEAD_DIM ** -0.5)).astype(bf16)
        qbd.append(qf)
        raw.append([jnp.dot(qb, k_refs[u][pg][...].astype(bf16), preferred_element_type=f32)
                    for pg in range(N_PAGES)])
        raw_new.append(lax.dot_general(qb, kn_ref[u].astype(bf16), _NT,
                                       preferred_element_type=f32))

    kmean = []
    for u in seqs:
        km = jnp.zeros((B_WIDTH, PAGE_SIZE), f32)
        for blk in range(N_PAST_BLOCKS):
            tot = k_refs[u][blk * PAGES_PER_BLOCK][...]
            for pg in range(blk * PAGES_PER_BLOCK + 1, (blk + 1) * PAGES_PER_BLOCK):
                tot = tot + k_refs[u][pg][...]
            mean = jnp.sum(tot, axis=1, keepdims=True) * (1.0 / MOBA_BLOCK)
            km = jnp.where(flane == blk, mean, km)
        kmean.append(km)
    gate = [jnp.dot(qbd[u], kmean[u], precision=lax.Precision.HIGHEST,
                    preferred_element_type=f32) for u in seqs]

    scores, s_new, m = [], [], []
    for u in seqs:
        sel = _select_blocks(gate[u], N_PAST_BLOCKS, N_PAST_BLOCKS, 1)
        blk_bias = jnp.where(sel, 0.0, NEG)
        sn = jnp.where(tk <= tq, raw_new[u] - slope * (tq - tk).astype(f32), NEG)
        mu = jnp.max(sn, axis=1, keepdims=True)
        su = []
        for pg in range(N_PAGES):
            dist = (tq + (PAST_LEN - pg * PAGE_SIZE) - kcol).astype(f32)
            blk = pg // PAGES_PER_BLOCK
            s = raw[u][pg] - slope * dist + blk_bias[:, blk:blk + 1]
            mu = jnp.maximum(mu, jnp.max(s, axis=1, keepdims=True))
            su.append(s)
        scores.append(su)
        s_new.append(sn)
        m.append(mu)

    for u in seqs:
        p_new = jnp.exp(s_new[u] - m[u])
        l = jnp.sum(p_new, axis=1, keepdims=True)
        acc = jnp.dot(p_new.astype(bf16), vn_ref[u].astype(bf16), preferred_element_type=f32)
        for pg in range(N_PAGES):
            pe = jnp.exp(scores[u][pg] - m[u])
            l = l + jnp.sum(pe, axis=1, keepdims=True)
            acc = acc + lax.dot_general(pe.astype(bf16), v_refs[u][pg][...].astype(bf16), _NT,
                                        preferred_element_type=f32)
        out = jnp.where(head_lanes, acc / l, 0.0)
        o_ref[u] = jnp.sum(out.reshape(DEC_SEQ, B_HEADS, B_WIDTH), axis=1)


def _moba_sample(page_table, q, k_new, v_new, cache_k, cache_v, layer):
    seq_spec = lambda r: pl.BlockSpec((SEQS_PER_STEP, r, B_WIDTH), lambda b, pt: (b, 0, 0))

    def page_spec(u, pg):
        return pl.BlockSpec(
            (None, None, B_WIDTH, PAGE_SIZE),
            lambda b, pt: (layer, pt[(b * SEQS_PER_STEP + u) * N_PAGES + pg], 0, 0))

    pages, operands = [], []
    for u in range(SEQS_PER_STEP):
        for cache in (cache_k, cache_v):
            pages += [page_spec(u, pg) for pg in range(N_PAGES)]
            operands += [cache] * N_PAGES
    return pl.pallas_call(
        _moba_sample_kernel,
        grid_spec=pltpu.PrefetchScalarGridSpec(
            num_scalar_prefetch=1,
            grid=(DEC_BATCH // SEQS_PER_STEP,),
            in_specs=[seq_spec(DEC_SEQ * B_HEADS), seq_spec(NEW_PAD), seq_spec(NEW_PAD)] + pages,
            out_specs=seq_spec(DEC_SEQ)),
        out_shape=jax.ShapeDtypeStruct((DEC_BATCH, DEC_SEQ, B_WIDTH), f32),
        compiler_params=_params(("parallel",)),
        name="moba_sample",
    )(page_table, q, k_new, v_new, *operands)


def _odd_kernel(x_ref, g_ref, wi_ref, cw_ref, wo_ref, st_ref,
                o_ref, cp_ref, cs_ref, gated_ref, ext_ref):
    i = pl.program_id(0)
    is_prompt = i < N_PROMPT_TILES
    h = _rms(x_ref[...], g_ref[...]).astype(bf16)

    @pl.when(i % TILES_PER_SEQ == 0)
    def _():
        ext_ref[CONV_FRONT - SUBLANES:CONV_FRONT, :] = jnp.zeros((SUBLANES, C_WIDTH), f32)

    @pl.when(jnp.logical_not(is_prompt))
    def _():
        ext_ref[:CONV_FRONT, :] = st_ref[...]

    def project(part, c):
        lo = part * C_WIDTH + c * CONV_CHUNK
        return jnp.dot(h, wi_ref[:, lo:lo + CONV_CHUNK], preferred_element_type=f32)

    for c in range(C_WIDTH // CONV_CHUNK):
        cs = slice(c * CONV_CHUNK, (c + 1) * CONV_CHUNK)
        cx = project(2, c) * project(0, c)
        ext_ref[CONV_FRONT:, cs] = cx
        back1 = jnp.where(is_prompt, ext_ref[CONV_FRONT - 1:CONV_FRONT - 1 + TM, cs],
                          ext_ref[CONV_FRONT - DEC_BATCH:CONV_FRONT - DEC_BATCH + TM, cs])
        back2 = jnp.where(is_prompt, ext_ref[CONV_FRONT - 2:CONV_FRONT - 2 + TM, cs],
                          ext_ref[:TM, cs])
        conv = cw_ref[0:1, cs] * back2 + cw_ref[1:2, cs] * back1 + cw_ref[2:3, cs] * cx
        gated_ref[:, cs] = (project(1, c) * conv).astype(bf16)
    o_ref[...] = x_ref[...] + jnp.dot(gated_ref[...], wo_ref[...], preferred_element_type=f32)

    @pl.when(is_prompt)
    def _():
        last = ext_ref[CONV_FRONT + TM - SUBLANES:, :]
        ext_ref[CONV_FRONT - SUBLANES:CONV_FRONT, :] = last
        cp_ref[0] = last

    @pl.when(i == N_TILES - 1)
    def _():
        cs_ref[...] = ext_ref[TM:, :]


def _odd(x, g, w_in, conv_w, w_out, state):
    return pl.pallas_call(
        _odd_kernel,
        grid=(N_TILES,),
        in_specs=[_tile_spec(D_MODEL), _const_spec((1, D_MODEL)),
                  _const_spec((D_MODEL, 3 * C_WIDTH)), _const_spec((CONV_W, C_WIDTH)),
                  _const_spec((C_WIDTH, D_MODEL)), _const_spec((CONV_FRONT, C_WIDTH))],
        out_specs=[_tile_spec(D_MODEL),
                   pl.BlockSpec((1, SUBLANES, C_WIDTH),
                                lambda i: (jnp.minimum(i // TILES_PER_SEQ, BATCH - 1), 0, 0)),
                   pl.BlockSpec((CONV_FRONT, C_WIDTH), lambda i: (0, 0))],
        out_shape=[jax.ShapeDtypeStruct((N_TOK, D_MODEL), f32),
                   jax.ShapeDtypeStruct((BATCH, SUBLANES, C_WIDTH), f32),
                   jax.ShapeDtypeStruct((CONV_FRONT, C_WIDTH), f32)],
        scratch_shapes=[pltpu.VMEM((TM, C_WIDTH), bf16),
                        pltpu.VMEM((CONV_FRONT + TM, C_WIDTH), f32)],
        compiler_params=_params(("arbitrary",)),
        name="odd_mixer",
    )(x, g, w_in, conv_w, w_out, state)


def _step_major(a):
    return jnp.swapaxes(a, 0, 1).reshape(a.shape[0] * a.shape[1], a.shape[2])


def _batch_major(a, steps=DEC_SEQ):
    return jnp.swapaxes(a.reshape(steps, DEC_BATCH, a.shape[-1]), 0, 1)


def kernel(x_prompt, x_sample, cache_k, cache_v, state_conv, page_table, ffn1_norm, ffn1_gate, ffn1_up, ffn1_down, mix_norm, ffn2_norm, ffn2_gate, ffn2_up, ffn2_down, even_w_in, even_w_out, a_v_gain, a_w_s, a_b_s, odd_w_in, odd_conv_w, odd_w_out, final_norm):
    n_phys = cache_k.shape[1]
    pages = lambda c: jnp.transpose(c, (0, 1, 3, 4, 2)).reshape(N_EVEN, n_phys, B_WIDTH, PAGE_SIZE)
    ck, cv = pages(cache_k), pages(cache_v)
    pt = page_table.reshape(-1).astype(jnp.int32)
    row = lambda a: a.reshape(1, -1)
    fg = row(final_norm)
    xs = (x_prompt.reshape(N_PROMPT, D_MODEL), _step_major(x_sample))

    k_s, v_s, conv_p, conv_s, av_s = [], [], [], [], []
    kv_stacks = ()
    for i in range(DEPTH):
        j = i // 2
        (x,) = _ffn(xs, row(ffn1_norm[i]), ffn1_gate, ffn1_up, ffn1_down, i, fg, first=(i == 0))
        if i % 2 == 0:
            reps = TM // A_CHUNK
            w4 = a_w_s[j][:, :DEC_SEQ, :DEC_SEQ]
            ws_tiled = jnp.stack([
                jnp.tile(a_w_s[j], (1, reps, reps)),
                jnp.repeat(jnp.repeat(w4, DEC_BATCH, axis=1), DEC_BATCH, axis=2)])
            bs_tiled = jnp.stack([
                jnp.tile(a_b_s[j], (1, reps)),
                jnp.repeat(a_b_s[j][:, :DEC_SEQ], DEC_BATCH, axis=1)])[..., None]
            w_in = even_w_in[j].astype(bf16)
            w_row = jnp.concatenate([w_in[:, :2 * A_WIDTH],
                                     w_in[:, 2 * A_WIDTH + B_WIDTH:2 * A_WIDTH + 2 * B_WIDTH]], axis=1)
            w_t = w_in[:, 2 * A_WIDTH:].T
            ya, va, q_t, k_stack, v_stack, vtb, kb, km, st = _even_in(
                x, row(mix_norm[i]), w_row, w_t, row(a_v_gain[j]), ws_tiled, bs_tiled, j, kv_stacks)
            kv_stacks = (k_stack, v_stack)
            kmean = km.reshape(N_TOK // MOBA_BLOCK, B_WIDTH)[:BATCH * N_BLOCKS]
            yb_t = _moba_prompt(q_t, kb, vtb, kmean.reshape(BATCH, N_BLOCKS, B_WIDTH))
            per_seq = jnp.transpose(st, (0, 3, 1, 2))
            padn = lambda a: jnp.pad(a, ((0, 0), (0, NEW_PAD - DEC_SEQ), (0, 0)))
            q_rep = jnp.repeat(per_seq[0], B_HEADS, axis=1)
            yb_s = _moba_sample(pt, q_rep, padn(per_seq[1]), padn(per_seq[2]), ck, cv, j)
            mix = (ya, yb_t, _step_major(yb_s).astype(bf16), even_w_out[j].astype(bf16))
            k_s.append(st[1].reshape(DEC_SEQ, B_HEADS, B_HEAD_DIM, DEC_BATCH))
            v_s.append(st[2].reshape(DEC_SEQ, B_HEADS, B_HEAD_DIM, DEC_BATCH))
            av_s.append(_batch_major(va))
        else:
            state = jnp.swapaxes(state_conv[j], 0, 1).reshape(CONV_FRONT, C_WIDTH)
            x, cp, cs = _odd(x, row(mix_norm[i]), odd_w_in[j].astype(bf16), odd_conv_w[j],
                             odd_w_out[j].astype(bf16), state)
            conv_p.append(cp[:, SUBLANES - (CONV_W - 1):])
            conv_s.append(_batch_major(cs, CONV_W - 1))
            mix = ()
        xs = _ffn((x,), row(ffn2_norm[i]), ffn2_gate, ffn2_up, ffn2_down, i, fg,
                  final=(i == DEPTH - 1), mix=mix)

    y_prompt = xs[0].reshape(BATCH, SEQ, D_MODEL)
    y_sample = _batch_major(xs[1])
    to_rows_p = lambda a: jnp.transpose(
        a.reshape(N_EVEN, BATCH, B_HEADS, B_HEAD_DIM, SEQ), (0, 1, 4, 2, 3))
    to_rows_s = lambda ts: jnp.transpose(jnp.stack(ts), (0, 4, 1, 2, 3))
    return (y_prompt, y_sample, to_rows_p(kv_stacks[0]), to_rows_p(kv_stacks[1]),
            jnp.stack(conv_p),
            to_rows_s(k_s), to_rows_s(v_s), jnp.stack(conv_s), jnp.stack(av_s))
```

```python
import functools
import struct

import jax
import jax.numpy as jnp
from jax import lax
from jax.experimental import pallas as pl
from jax.experimental.pallas import tpu as pltpu

D_MODEL = 1024
BATCH = 4
SEQ = 4096
DEPTH = 4
DEC_BATCH = 128
DEC_SEQ = 4
PAST_LEN = 2048
PAGE_SIZE = 128
N_EVEN = (DEPTH + 1) // 2
N_ODD = DEPTH // 2
D_FF = 2816
RMS_EPS = 1e-6
A_WIDTH = D_MODEL // 2
A_CHUNK = 128
A_GROUPS = 4
A_GROUP_DIM = A_WIDTH // A_GROUPS
B_HEAD_DIM = 64
B_WIDTH = D_MODEL - A_WIDTH
B_HEADS = B_WIDTH // B_HEAD_DIM
MOBA_BLOCK = 256
MOBA_TOPK = 3
C_WIDTH = D_MODEL
CONV_W = 3

N_PROMPT = BATCH * SEQ
N_SAMPLE = DEC_BATCH * DEC_SEQ
N_TOK = N_PROMPT + N_SAMPLE
TM = 512
N_TILES = N_TOK // TM
N_PROMPT_TILES = N_PROMPT // TM
TILES_PER_SEQ = SEQ // TM
FF_CHUNK = 256
N_FF_CHUNKS = D_FF // FF_CHUNK
N_PAGES = PAST_LEN // PAGE_SIZE
N_BLOCKS = SEQ // MOBA_BLOCK
N_PAST_BLOCKS = PAST_LEN // MOBA_BLOCK
PAGES_PER_BLOCK = MOBA_BLOCK // PAGE_SIZE
BLOCKS_PER_TILE = TM // MOBA_BLOCK
NEW_PAD = 16
HEAD_PAIR = 2 * B_HEAD_DIM
N_HEAD_PAIRS = B_HEADS // 2
SLOPES = tuple(2.0 ** (-8.0 * (h + 1) / B_HEADS) for h in range(B_HEADS))
NEG = -1e30
SUBLANES = 8
LANES = 128
CONV_FRONT = 2 * DEC_BATCH
CONV_CHUNK = 256
VMEM_LIMIT = 56 * 1024 * 1024

assert N_SAMPLE == TM and N_TOK % TM == 0 and D_FF % FF_CHUNK == 0
assert DEC_BATCH == LANES and CONV_W - 1 == 2 and DEC_SEQ >= CONV_W - 1

_NT = (((1,), (1,)), ((), ()))
_TN = (((0,), (0,)), ((), ()))
bf16 = jnp.bfloat16
f32 = jnp.float32


def _log2(n):
    assert n & (n - 1) == 0
    return n.bit_length() - 1


def _rms(x, g):
    ms = jnp.mean(x * x, axis=-1, keepdims=True)
    return x * lax.rsqrt(ms + RMS_EPS) * g


def _params(sem):
    return pltpu.CompilerParams(dimension_semantics=sem, vmem_limit_bytes=VMEM_LIMIT)


def _const_spec(shape):
    nd = len(shape)
    return pl.BlockSpec(shape, lambda *_: (0,) * nd, pipeline_mode=pl.Buffered(1))


def _tile_spec(width):
    return pl.BlockSpec((TM, width), lambda i: (i, 0))


def _prompt_tile_spec(width):
    return pl.BlockSpec((TM, width), lambda i: (jnp.minimum(i, N_PROMPT_TILES - 1), 0))


def _feature_major_index(i):
    j = jnp.minimum(i, N_PROMPT_TILES - 1)
    return (j // TILES_PER_SEQ, 0, j % TILES_PER_SEQ)


def _feature_major_spec():
    return pl.BlockSpec((1, B_WIDTH, TM), _feature_major_index)


def _layered_feature_major_spec(layer, n_slots):
    assert layer % n_slots == 0
    return pl.BlockSpec((n_slots, 1, B_WIDTH, TM),
                        lambda i: (layer // n_slots,) + _feature_major_index(i))


def _ffn_kernel(*refs, first, final, mixed, layer):
    n_x = (2 if first else 1) + (4 if mixed else 0)
    x_refs, (g_ref, wg_hbm, wu_hbm, wd_hbm, fg_ref) = refs[:n_x], refs[n_x:n_x + 5]
    n_out = 2 if final else 1
    o_refs = refs[n_x + 5:n_x + 5 + n_out]
    scratch = list(refs[n_x + 5 + n_out:])
    xin_ref = scratch.pop(0) if mixed else None
    acc_ref, wg_ref, wu_ref, wd_ref, sem = scratch
    i = pl.program_id(0)
    is_prompt = i < N_PROMPT_TILES

    def chunk_copies(c):
        sl = slice(c * FF_CHUNK, (c + 1) * FF_CHUNK)
        return (pltpu.make_async_copy(wg_hbm.at[layer, :, sl], wg_ref.at[:, sl], sem.at[0, c]),
                pltpu.make_async_copy(wu_hbm.at[layer, :, sl], wu_ref.at[:, sl], sem.at[1, c]),
                pltpu.make_async_copy(wd_hbm.at[layer, sl, :], wd_ref.at[sl, :], sem.at[2, c]))

    @pl.when(i == 0)
    def _():
        for c in range(N_FF_CHUNKS):
            for copy in chunk_copies(c):
                copy.start()

    if first:
        x = jnp.where(is_prompt, x_refs[0][...], x_refs[1][...])
    elif mixed:
        x_ref, ya_ref, ybt_ref, ybs_ref, wo_ref = x_refs
        base = x_ref[...] + jnp.dot(ya_ref[...], wo_ref[:A_WIDTH, :], preferred_element_type=f32)

        @pl.when(is_prompt)
        def _():
            xin_ref[...] = base + lax.dot_general(ybt_ref[0], wo_ref[A_WIDTH:, :], _TN,
                                                  preferred_element_type=f32)

        @pl.when(jnp.logical_not(is_prompt))
        def _():
            xin_ref[...] = base + jnp.dot(ybs_ref[...], wo_ref[A_WIDTH:, :],
                                          preferred_element_type=f32)

        x = xin_ref[...]
    else:
        x = x_refs[0][...]

    def body(wait_for_weights):
        h = _rms(x, g_ref[...]).astype(bf16)
        for c in range(N_FF_CHUNKS):
            sl = slice(c * FF_CHUNK, (c + 1) * FF_CHUNK)
            if wait_for_weights:
                for copy in chunk_copies(c):
                    copy.wait()
            gate = jnp.dot(h, wg_ref[:, sl].astype(bf16), preferred_element_type=f32)
            up = jnp.dot(h, wu_ref[:, sl].astype(bf16), preferred_element_type=f32)
            act = (gate / (1.0 + jnp.exp(-gate)) * up).astype(bf16)
            down = jnp.dot(act, wd_ref[sl, :].astype(bf16), preferred_element_type=f32)
            if c == 0:
                acc_ref[...] = down
            else:
                acc_ref[...] += down
        y = x + 0.5 * acc_ref[...]
        if final:
            y = _rms(y, fg_ref[...])

            @pl.when(is_prompt)
            def _():
                o_refs[0][...] = y

            @pl.when(jnp.logical_not(is_prompt))
            def _():
                o_refs[1][...] = y
        else:
            o_refs[0][...] = y

    @pl.when(i == 0)
    def _():
        body(True)

    @pl.when(i != 0)
    def _():
        body(False)


def _ffn(xs, g, wg, wu, wd, layer, fg, first=False, final=False, mix=()):
    in_hbm = pl.BlockSpec(memory_space=pl.ANY)
    sample_spec = pl.BlockSpec((TM, D_MODEL), lambda i: (0, 0))
    x_specs = [_prompt_tile_spec(D_MODEL), sample_spec] if first else [_tile_spec(D_MODEL)]
    scratch = [pltpu.VMEM((TM, D_MODEL), f32),
               pltpu.VMEM((D_MODEL, D_FF), f32), pltpu.VMEM((D_MODEL, D_FF), f32),
               pltpu.VMEM((D_FF, D_MODEL), f32),
               pltpu.SemaphoreType.DMA((3, N_FF_CHUNKS))]
    if mix:
        x_specs += [_tile_spec(A_WIDTH), _feature_major_spec(),
                    _const_spec((TM, B_WIDTH)), _const_spec((D_MODEL, D_MODEL))]
        scratch = [pltpu.VMEM((TM, D_MODEL), f32)] + scratch
    if final:
        out_specs = [_prompt_tile_spec(D_MODEL), sample_spec]
        out_shape = [jax.ShapeDtypeStruct((N_PROMPT, D_MODEL), f32),
                     jax.ShapeDtypeStruct((N_SAMPLE, D_MODEL), f32)]
    else:
        out_specs = [_tile_spec(D_MODEL)]
        out_shape = [jax.ShapeDtypeStruct((N_TOK, D_MODEL), f32)]
    return pl.pallas_call(
        functools.partial(_ffn_kernel, first=first, final=final, mixed=bool(mix), layer=layer),
        grid=(N_TILES,),
        in_specs=x_specs + [_const_spec((1, D_MODEL)), in_hbm, in_hbm, in_hbm,
                            _const_spec((1, D_MODEL))],
        out_specs=out_specs,
        out_shape=out_shape,
        scratch_shapes=scratch,
        compiler_params=_params(("arbitrary",)),
        name="ffn",
    )(*xs, *mix, g, wg, wu, wd, fg)


ROW_COLS = 2 * A_WIDTH + B_WIDTH
T_ROWS = 3 * B_WIDTH


def _even_in_kernel(*refs, n_prev):
    (x_ref, g_ref, w_ref, wt_ref, gain_ref, ws_ref, bs_ref) = refs[:7]
    (ya_ref, vs_ref, qt_ref, kt_ref, vt_ref, vtb_ref, kb_ref, km_ref, st_ref,
     z_ref, h_ref) = refs[7 + n_prev:]
    i = pl.program_id(0)
    is_prompt = i < N_PROMPT_TILES
    h = _rms(x_ref[...], g_ref[...]).astype(bf16)
    h_ref[...] = h
    z_ref[...] = jnp.dot(h, w_ref[...], preferred_element_type=f32)

    def project_t(part):
        rows = slice(part * B_WIDTH, (part + 1) * B_WIDTH)
        return lax.dot_general(wt_ref[rows, :], h_ref[...], _NT, preferred_element_type=f32)

    def gelu(a):
        return a * (0.5 * (1.0 + jnp.tanh(0.7978845608028654 * (a + 0.044715 * (a * a * a)))))

    row = lax.broadcasted_iota(jnp.int32, (TM, TM), 0)
    col = lax.broadcasted_iota(jnp.int32, (TM, TM), 1)
    chunk = lambda r: jnp.where(is_prompt, r >> _log2(A_CHUNK), r & (DEC_BATCH - 1))
    keep = (chunk(row) == chunk(col)) & (col <= row)
    for g in range(A_GROUPS):
        gs = slice(g * A_GROUP_DIM, (g + 1) * A_GROUP_DIM)
        u = gelu(z_ref[:, gs])
        v = gelu(z_ref[:, A_WIDTH + g * A_GROUP_DIM:A_WIDTH + (g + 1) * A_GROUP_DIM])
        vn = _rms(v, gain_ref[:, gs])
        w = jnp.where(keep, ws_ref[0, g], 0.0).astype(bf16)
        mixed = jnp.dot(w, vn.astype(bf16), preferred_element_type=f32) + bs_ref[0, g]
        ya_ref[:, gs] = (u * mixed).astype(bf16)

        @pl.when(i == N_TILES - 1)
        def _():
            vs_ref[:, gs] = vn

    k = z_ref[:, 2 * A_WIDTH:]
    kb_ref[...] = k.astype(bf16)
    for blk in range(BLOCKS_PER_TILE):
        ks = jnp.sum(k[blk * MOBA_BLOCK:(blk + 1) * MOBA_BLOCK], axis=0, keepdims=True)
        km_ref[0, blk:blk + 1, :] = ks * (1.0 / MOBA_BLOCK)

    @pl.when(is_prompt)
    def _():
        qt_ref[0] = project_t(0)
        kt = project_t(1)
        for slot in range(kt_ref.shape[0]):
            kt_ref[slot, 0] = kt
        vt = project_t(2)
        for slot in range(vt_ref.shape[0]):
            vt_ref[slot, 0] = vt
        vtb_ref[0] = vt.astype(bf16)

    @pl.when(i == N_TILES - 1)
    def _():
        for part in range(3):
            zt = project_t(part)
            for t in range(DEC_SEQ):
                st_ref[part, t] = zt[:, t * DEC_BATCH:(t + 1) * DEC_BATCH]


def _even_in(x, g, w_row, w_t, gain, ws_tiled, bs_tiled, layer, kv_stacks):
    kind = lambda i: (i // N_PROMPT_TILES, 0, 0, 0)
    fm = lambda dt: jax.ShapeDtypeStruct((BATCH, B_WIDTH, SEQ), dt)
    stack = jax.ShapeDtypeStruct((N_EVEN, BATCH, B_WIDTH, SEQ), f32)
    n_in = 7
    stack_spec = _layered_feature_major_spec(layer, 1 if kv_stacks else N_EVEN)
    return pl.pallas_call(
        functools.partial(_even_in_kernel, n_prev=len(kv_stacks)),
        grid=(N_TILES,),
        in_specs=[_tile_spec(D_MODEL), _const_spec((1, D_MODEL)),
                  _const_spec((D_MODEL, ROW_COLS)), _const_spec((T_ROWS, D_MODEL)),
                  _const_spec((1, A_WIDTH)),
                  pl.BlockSpec((1, A_GROUPS, TM, TM), kind),
                  pl.BlockSpec((1, A_GROUPS, TM, 1), kind)]
        + [pl.BlockSpec(memory_space=pl.ANY)] * len(kv_stacks),
        out_specs=[_tile_spec(A_WIDTH),
                   pl.BlockSpec((TM, A_WIDTH), lambda i: (0, 0)),
                   _feature_major_spec(), stack_spec, stack_spec,
                   _feature_major_spec(),
                   _tile_spec(B_WIDTH),
                   pl.BlockSpec((1, BLOCKS_PER_TILE, B_WIDTH), lambda i: (i, 0, 0)),
                   pl.BlockSpec((3, DEC_SEQ, B_WIDTH, DEC_BATCH), lambda i: (0, 0, 0, 0))],
        out_shape=[jax.ShapeDtypeStruct((N_TOK, A_WIDTH), bf16),
                   jax.ShapeDtypeStruct((N_SAMPLE, A_WIDTH), f32),
                   fm(f32), stack, stack, fm(bf16),
                   jax.ShapeDtypeStruct((N_TOK, B_WIDTH), bf16),
                   jax.ShapeDtypeStruct((N_TILES, BLOCKS_PER_TILE, B_WIDTH), f32),
                   jax.ShapeDtypeStruct((3, DEC_SEQ, B_WIDTH, DEC_BATCH), f32)],
        scratch_shapes=[pltpu.VMEM((TM, ROW_COLS), f32), pltpu.VMEM((TM, D_MODEL), bf16)],
        input_output_aliases={n_in + s: 3 + s for s in range(len(kv_stacks))},
        compiler_params=_params(("arbitrary",)),
        name="even_in",
    )(x, g, w_row, w_t, gain, ws_tiled, bs_tiled, *kv_stacks)


def _select_blocks(gate, n_valid, n_past, axis):
    blk = lax.broadcasted_iota(jnp.int32, gate.shape, axis)
    valid = blk < n_valid
    g = jnp.where(valid, gate, -jnp.inf)
    sel = jnp.zeros(gate.shape, jnp.bool_)
    for _ in range(min(MOBA_TOPK, n_past)):
        mx = jnp.max(g, axis=axis, keepdims=True)
        idx = jnp.min(jnp.where(g == mx, blk, n_past), axis=axis, keepdims=True)
        hit = blk == idx
        sel = sel | (hit & valid)
        g = jnp.where(hit, -jnp.inf, g)
    return sel


PAIR_COLS = 2 * MOBA_BLOCK
ALL_COLS = B_HEADS * MOBA_BLOCK


LOG2E = 1.4426950408889634


def _round_bf16(x):
    bits = struct.unpack("<I", struct.pack("<f", x))[0]
    bits = (bits + 0x7FFF + ((bits >> 16) & 1)) & 0xFFFF0000
    return struct.unpack("<f", struct.pack("<I", bits))[0]


def _bf16_terms(x, n):
    terms = []
    for _ in range(n):
        t = _round_bf16(x)
        terms.append(t)
        x -= t
    return terms


SLOPE_TERMS = 4
SUM_ROWS = 16
KEY_UNROLL = 2


def _moba_prompt_kernel(q_ref, k_ref, v_ref, km_ref, o_ref,
                        kpos_ref, qbd_ref, bias_ref, s0_ref, s1_ref, m_ref, acc_ref):
    b = pl.program_id(0)
    i = pl.program_id(1)

    @pl.when((b == 0) & (i == 0))
    def _():
        key = lax.broadcasted_iota(jnp.int32, (MOBA_BLOCK, HEAD_PAIR), 0)
        col = lax.broadcasted_iota(jnp.int32, (MOBA_BLOCK, HEAD_PAIR), 1)
        kpos_ref[...] = jnp.where(col < SLOPE_TERMS, key, 0).astype(bf16)
        r = lax.broadcasted_iota(jnp.int32, (HEAD_PAIR, PAIR_COLS), 0)
        e_of_col = lax.broadcasted_iota(jnp.int32, (HEAD_PAIR, PAIR_COLS), 1) >> _log2(MOBA_BLOCK)
        for p in range(N_HEAD_PAIRS):
            slope_rows = jnp.zeros((HEAD_PAIR, PAIR_COLS), f32)
            for e in range(2):
                for t, term in enumerate(_bf16_terms(SLOPES[2 * p + e] * LOG2E, SLOPE_TERMS)):
                    slope_rows = jnp.where((r == t) & (e_of_col == e), term, slope_rows)
            qbd_ref[p, HEAD_PAIR:, :] = slope_rows.astype(bf16)

    feat = lax.broadcasted_iota(jnp.int32, (HEAD_PAIR, PAIR_COLS), 0)
    qcol = lax.broadcasted_iota(jnp.int32, (HEAD_PAIR, PAIR_COLS), 1)
    mine = (feat >> _log2(B_HEAD_DIM)) == (qcol >> _log2(MOBA_BLOCK))
    blk = lax.broadcasted_iota(jnp.int32, (N_BLOCKS, PAIR_COLS), 0)
    bcol = lax.broadcasted_iota(jnp.int32, (N_BLOCKS, PAIR_COLS), 1)
    blocks_back = (i - blk).astype(f32) * float(MOBA_BLOCK)
    for p in range(N_HEAD_PAIRS):
        ls = slice(p * HEAD_PAIR, (p + 1) * HEAD_PAIR)
        q2 = q_ref[0, ls, :]
        qmf = jnp.where(mine, jnp.concatenate([q2, q2], axis=1), 0.0)
        qbd_ref[p, :HEAD_PAIR, :] = (qmf * (LOG2E * B_HEAD_DIM ** -0.5)).astype(bf16)
        gate = jnp.dot(km_ref[0, :, ls], qmf, precision=lax.Precision.HIGHEST,
                       preferred_element_type=f32)
        sel = _select_blocks(gate, i, N_BLOCKS - 1, 0)
        slope = jnp.where(bcol < MOBA_BLOCK, SLOPES[2 * p] * LOG2E, SLOPES[2 * p + 1] * LOG2E)
        bias = jnp.where(sel, -slope * blocks_back, NEG)
        bias_ref[:, p * PAIR_COLS:(p + 1) * PAIR_COLS] = jnp.where(blk == i, 0.0, bias)
    m_ref[...] = jnp.full((1, ALL_COLS), NEG, f32)
    acc_ref[...] = jnp.zeros((B_HEADS, B_HEAD_DIM + SUM_ROWS, MOBA_BLOCK), f32)

    def scores(n, s_ref, own):
        start = pl.multiple_of(n * MOBA_BLOCK, MOBA_BLOCK)
        for p in range(N_HEAD_PAIRS):
            kn = k_ref[pl.ds(start, MOBA_BLOCK), p * HEAD_PAIR:(p + 1) * HEAD_PAIR]
            kn = jnp.concatenate([kn, kpos_ref[...]], axis=1)
            s = jnp.dot(kn, qbd_ref[p], preferred_element_type=f32)
            if own:
                key = lax.broadcasted_iota(jnp.int32, (MOBA_BLOCK, PAIR_COLS), 0)
                query = lax.broadcasted_iota(jnp.int32, (MOBA_BLOCK, PAIR_COLS), 1) & (MOBA_BLOCK - 1)
                s = jnp.where(key <= query, s, NEG)
            s_ref[p] = s

    def absorb(n, s_ref):
        start = pl.multiple_of(n * MOBA_BLOCK, MOBA_BLOCK)
        for p in range(N_HEAD_PAIRS):
            cs = slice(p * PAIR_COLS, (p + 1) * PAIR_COLS)
            s = s_ref[p]
            brow = bias_ref[pl.ds(n, 1), cs]
            m_old = m_ref[:, cs]
            m_new = jnp.maximum(m_old, jnp.max(s, axis=0, keepdims=True) + brow)
            alpha = jnp.exp2(m_old - m_new)
            peb = jnp.exp2(s - (m_new - brow)).astype(bf16)
            m_ref[:, cs] = m_new
            for e in range(2):
                h = 2 * p + e
                es = slice(e * MOBA_BLOCK, (e + 1) * MOBA_BLOCK)
                vn = v_ref[0, h * B_HEAD_DIM:(h + 1) * B_HEAD_DIM, pl.ds(start, MOBA_BLOCK)]
                vn = jnp.concatenate([vn, jnp.ones((SUM_ROWS, MOBA_BLOCK), bf16)], axis=0)
                acc_ref[h] = alpha[:, es] * acc_ref[h] + jnp.dot(
                    vn, peb[:, es], preferred_element_type=f32)

    slots = (s0_ref, s1_ref)
    scores(i, s0_ref, True)

    def steps(first, count):
        for u in range(count):
            before = jnp.where(first == 0, i, first - 1) if u == 0 else first + u - 1
            scores(first + u, slots[(u + 1) % 2], False)
            absorb(before, slots[u % 2])

    def trip(jj, carry):
        steps(KEY_UNROLL * jj, KEY_UNROLL)
        return carry

    lax.fori_loop(0, i // KEY_UNROLL, trip, 0)
    done = (i // KEY_UNROLL) * KEY_UNROLL
    for rest in range(KEY_UNROLL):
        @pl.when(i - done == rest)
        def _():
            steps(done, rest)
            absorb(jnp.maximum(i - 1, 0), slots[rest % 2])

    for h in range(B_HEADS):
        o_ref[0, h * B_HEAD_DIM:(h + 1) * B_HEAD_DIM, :] = (
            acc_ref[h, :B_HEAD_DIM, :] / acc_ref[h, B_HEAD_DIM:B_HEAD_DIM + 1, :]).astype(bf16)


def _moba_prompt(q_t, kb, vtb, kmean):
    qspec = pl.BlockSpec((1, B_WIDTH, MOBA_BLOCK), lambda b, i: (b, 0, i))
    return pl.pallas_call(
        _moba_prompt_kernel,
        grid=(BATCH, N_BLOCKS),
        in_specs=[qspec,
                  pl.BlockSpec((SEQ, B_WIDTH), lambda b, i: (b, 0)),
                  pl.BlockSpec((1, B_WIDTH, SEQ), lambda b, i: (b, 0, 0)),
                  pl.BlockSpec((1, N_BLOCKS, B_WIDTH), lambda b, i: (b, 0, 0))],
        out_specs=qspec,
        out_shape=jax.ShapeDtypeStruct((BATCH, B_WIDTH, SEQ), bf16),
        scratch_shapes=[pltpu.VMEM((MOBA_BLOCK, HEAD_PAIR), bf16),
                        pltpu.VMEM((N_HEAD_PAIRS, 2 * HEAD_PAIR, PAIR_COLS), bf16),
                        pltpu.VMEM((N_BLOCKS, ALL_COLS), f32),
                        pltpu.VMEM((N_HEAD_PAIRS, MOBA_BLOCK, PAIR_COLS), f32),
                        pltpu.VMEM((N_HEAD_PAIRS, MOBA_BLOCK, PAIR_COLS), f32),
                        pltpu.VMEM((1, ALL_COLS), f32),
                        pltpu.VMEM((B_HEADS, B_HEAD_DIM + SUM_ROWS, MOBA_BLOCK), f32)],
        compiler_params=_params(("arbitrary", "arbitrary")),
        name="moba_prompt",
    )(q_t, kb, vtb, kmean)


SEQS_PER_STEP = 2


def _moba_sample_kernel(pt_ref, q_ref, kn_ref, vn_ref, *refs):
    seqs = range(SEQS_PER_STEP)
    k_refs = [refs[2 * u * N_PAGES:(2 * u + 1) * N_PAGES] for u in seqs]
    v_refs = [refs[(2 * u + 1) * N_PAGES:(2 * u + 2) * N_PAGES] for u in seqs]
    o_ref = refs[2 * SEQS_PER_STEP * N_PAGES]
    rows = DEC_SEQ * B_HEADS
    row = lax.broadcasted_iota(jnp.int32, (rows, B_WIDTH), 0)
    lane = lax.broadcasted_iota(jnp.int32, (rows, B_WIDTH), 1)
    head_lanes = (lane >> _log2(B_HEAD_DIM)) == (row & (B_HEADS - 1))
    r1 = lax.broadcasted_iota(jnp.int32, (rows, 1), 0)
    hrow = r1 & (B_HEADS - 1)
    slope = jnp.zeros((rows, 1), f32)
    for h in range(B_HEADS):
        slope = jnp.where(hrow == h, SLOPES[h], slope)
    tq = r1 >> _log2(B_HEADS)
    kcol = lax.broadcasted_iota(jnp.int32, (rows, PAGE_SIZE), 1)
    tk = lax.broadcasted_iota(jnp.int32, (rows, NEW_PAD), 1)
    flane = lax.broadcasted_iota(jnp.int32, (B_WIDTH, PAGE_SIZE), 1)

    qbd, raw, raw_new = [], [], []
    for u in seqs:
        qf = jnp.where(head_lanes, q_ref[u], 0.0)
        qb = (qf * (B_HEAD_DIM ** -0.5)).astype(bf16)
        qbd.append(qf)
        raw.append([jnp.dot(qb, k_refs[u][pg][...].astype(bf16), preferred_element_type=f32)
                    for pg in range(N_PAGES)])
        raw_new.append(lax.dot_general(qb, kn_ref[u].astype(bf16), _NT,
                                       preferred_element_type=f32))

    kmean = []
    for u in seqs:
        km = jnp.zeros((B_WIDTH, PAGE_SIZE), f32)
        for blk in range(N_PAST_BLOCKS):
            tot = k_refs[u][blk * PAGES_PER_BLOCK][...]
            for pg in range(blk * PAGES_PER_BLOCK + 1, (blk + 1) * PAGES_PER_BLOCK):
                tot = tot + k_refs[u][pg][...]
            mean = jnp.sum(tot, axis=1, keepdims=True) * (1.0 / MOBA_BLOCK)
            km = jnp.where(flane == blk, mean, km)
        kmean.append(km)
    gate = [jnp.dot(qbd[u], kmean[u], precision=lax.Precision.HIGHEST,
                    preferred_element_type=f32) for u in seqs]

    scores, s_new, m = [], [], []
    for u in seqs:
        sel = _select_blocks(gate[u], N_PAST_BLOCKS, N_PAST_BLOCKS, 1)
        blk_bias = jnp.where(sel, 0.0, NEG)
        sn = jnp.where(tk <= tq, raw_new[u] - slope * (tq - tk).astype(f32), NEG)
        mu = jnp.max(sn, axis=1, keepdims=True)
        su = []
        for pg in range(N_PAGES):
            dist = (tq + (PAST_LEN - pg * PAGE_SIZE) - kcol).astype(f32)
            blk = pg // PAGES_PER_BLOCK
            s = raw[u][pg] - slope * dist + blk_bias[:, blk:blk + 1]
            mu = jnp.maximum(mu, jnp.max(s, axis=1, keepdims=True))
            su.append(s)
        scores.append(su)
        s_new.append(sn)
        m.append(mu)

    for u in seqs:
        p_new = jnp.exp(s_new[u] - m[u])
        l = jnp.sum(p_new, axis=1, keepdims=True)
        acc = jnp.dot(p_new.astype(bf16), vn_ref[u].astype(bf16), preferred_element_type=f32)
        for pg in range(N_PAGES):
            pe = jnp.exp(scores[u][pg] - m[u])
            l = l + jnp.sum(pe, axis=1, keepdims=True)
            acc = acc + lax.dot_general(pe.astype(bf16), v_refs[u][pg][...].astype(bf16), _NT,
                                        preferred_element_type=f32)
        out = jnp.where(head_lanes, acc / l, 0.0)
        o_ref[u] = jnp.sum(out.reshape(DEC_SEQ, B_HEADS, B_WIDTH), axis=1)


def _moba_sample(page_table, q, k_new, v_new, cache_k, cache_v, layer):
    seq_spec = lambda r: pl.BlockSpec((SEQS_PER_STEP, r, B_WIDTH), lambda b, pt: (b, 0, 0))

    def page_spec(u, pg):
        return pl.BlockSpec(
            (None, None, B_WIDTH, PAGE_SIZE),
            lambda b, pt: (layer, pt[(b * SEQS_PER_STEP + u) * N_PAGES + pg], 0, 0))

    pages, operands = [], []
    for u in range(SEQS_PER_STEP):
        for cache in (cache_k, cache_v):
            pages += [page_spec(u, pg) for pg in range(N_PAGES)]
            operands += [cache] * N_PAGES
    return pl.pallas_call(
        _moba_sample_kernel,
        grid_spec=pltpu.PrefetchScalarGridSpec(
            num_scalar_prefetch=1,
            grid=(DEC_BATCH // SEQS_PER_STEP,),
            in_specs=[seq_spec(DEC_SEQ * B_HEADS), seq_spec(NEW_PAD), seq_spec(NEW_PAD)] + pages,
            out_specs=seq_spec(DEC_SEQ)),
        out_shape=jax.ShapeDtypeStruct((DEC_BATCH, DEC_SEQ, B_WIDTH), f32),
        compiler_params=_params(("parallel",)),
        name="moba_sample",
    )(page_table, q, k_new, v_new, *operands)


def _odd_kernel(x_ref, g_ref, wi_ref, cw_ref, wo_ref, st_ref,
                o_ref, cp_ref, cs_ref, gated_ref, ext_ref):
    i = pl.program_id(0)
    is_prompt = i < N_PROMPT_TILES
    h = _rms(x_ref[...], g_ref[...]).astype(bf16)

    @pl.when(i % TILES_PER_SEQ == 0)
    def _():
        ext_ref[CONV_FRONT - SUBLANES:CONV_FRONT, :] = jnp.zeros((SUBLANES, C_WIDTH), f32)

    @pl.when(jnp.logical_not(is_prompt))
    def _():
        ext_ref[:CONV_FRONT, :] = st_ref[...]

    def project(part, c):
        lo = part * C_WIDTH + c * CONV_CHUNK
        return jnp.dot(h, wi_ref[:, lo:lo + CONV_CHUNK], preferred_element_type=f32)

    for c in range(C_WIDTH // CONV_CHUNK):
        cs = slice(c * CONV_CHUNK, (c + 1) * CONV_CHUNK)
        cx = project(2, c) * project(0, c)
        ext_ref[CONV_FRONT:, cs] = cx
        back1 = jnp.where(is_prompt, ext_ref[CONV_FRONT - 1:CONV_FRONT - 1 + TM, cs],
                          ext_ref[CONV_FRONT - DEC_BATCH:CONV_FRONT - DEC_BATCH + TM, cs])
        back2 = jnp.where(is_prompt, ext_ref[CONV_FRONT - 2:CONV_FRONT - 2 + TM, cs],
                          ext_ref[:TM, cs])
        conv = cw_ref[0:1, cs] * back2 + cw_ref[1:2, cs] * back1 + cw_ref[2:3, cs] * cx
        gated_ref[:, cs] = (project(1, c) * conv).astype(bf16)
    o_ref[...] = x_ref[...] + jnp.dot(gated_ref[...], wo_ref[...], preferred_element_type=f32)

    @pl.when(is_prompt)
    def _():
        last = ext_ref[CONV_FRONT + TM - SUBLANES:, :]
        ext_ref[CONV_FRONT - SUBLANES:CONV_FRONT, :] = last
        cp_ref[0] = last

    @pl.when(i == N_TILES - 1)
    def _():
        cs_ref[...] = ext_ref[TM:, :]


def _odd(x, g, w_in, conv_w, w_out, state):
    return pl.pallas_call(
        _odd_kernel,
        grid=(N_TILES,),
        in_specs=[_tile_spec(D_MODEL), _const_spec((1, D_MODEL)),
                  _const_spec((D_MODEL, 3 * C_WIDTH)), _const_spec((CONV_W, C_WIDTH)),
                  _const_spec((C_WIDTH, D_MODEL)), _const_spec((CONV_FRONT, C_WIDTH))],
        out_specs=[_tile_spec(D_MODEL),
                   pl.BlockSpec((1, SUBLANES, C_WIDTH),
                                lambda i: (jnp.minimum(i // TILES_PER_SEQ, BATCH - 1), 0, 0)),
                   pl.BlockSpec((CONV_FRONT, C_WIDTH), lambda i: (0, 0))],
        out_shape=[jax.ShapeDtypeStruct((N_TOK, D_MODEL), f32),
                   jax.ShapeDtypeStruct((BATCH, SUBLANES, C_WIDTH), f32),
                   jax.ShapeDtypeStruct((CONV_FRONT, C_WIDTH), f32)],
        scratch_shapes=[pltpu.VMEM((TM, C_WIDTH), bf16),
                        pltpu.VMEM((CONV_FRONT + TM, C_WIDTH), f32)],
        compiler_params=_params(("arbitrary",)),
        name="odd_mixer",
    )(x, g, w_in, conv_w, w_out, state)


def _step_major(a):
    return jnp.swapaxes(a, 0, 1).reshape(a.shape[0] * a.shape[1], a.shape[2])


def _batch_major(a, steps=DEC_SEQ):
    return jnp.swapaxes(a.reshape(steps, DEC_BATCH, a.shape[-1]), 0, 1)


def kernel(x_prompt, x_sample, cache_k, cache_v, state_conv, page_table, ffn1_norm, ffn1_gate, ffn1_up, ffn1_down, mix_norm, ffn2_norm, ffn2_gate, ffn2_up, ffn2_down, even_w_in, even_w_out, a_v_gain, a_w_s, a_b_s, odd_w_in, odd_conv_w, odd_w_out, final_norm):
    n_phys = cache_k.shape[1]
    pages = lambda c: jnp.transpose(c, (0, 1, 3, 4, 2)).reshape(N_EVEN, n_phys, B_WIDTH, PAGE_SIZE)
    ck, cv = pages(cache_k), pages(cache_v)
    pt = page_table.reshape(-1).astype(jnp.int32)
    row = lambda a: a.reshape(1, -1)
    fg = row(final_norm)
    xs = (x_prompt.reshape(N_PROMPT, D_MODEL), _step_major(x_sample))

    k_s, v_s, conv_p, conv_s, av_s = [], [], [], [], []
    kv_stacks = ()
    for i in range(DEPTH):
        j = i // 2
        (x,) = _ffn(xs, row(ffn1_norm[i]), ffn1_gate, ffn1_up, ffn1_down, i, fg, first=(i == 0))
        if i % 2 == 0:
            reps = TM // A_CHUNK
            w4 = a_w_s[j][:, :DEC_SEQ, :DEC_SEQ]
            ws_tiled = jnp.stack([
                jnp.tile(a_w_s[j], (1, reps, reps)),
                jnp.repeat(jnp.repeat(w4, DEC_BATCH, axis=1), DEC_BATCH, axis=2)])
            bs_tiled = jnp.stack([
                jnp.tile(a_b_s[j], (1, reps)),
                jnp.repeat(a_b_s[j][:, :DEC_SEQ], DEC_BATCH, axis=1)])[..., None]
            w_in = even_w_in[j].astype(bf16)
            w_row = jnp.concatenate([w_in[:, :2 * A_WIDTH],
                                     w_in[:, 2 * A_WIDTH + B_WIDTH:2 * A_WIDTH + 2 * B_WIDTH]], axis=1)
            w_t = w_in[:, 2 * A_WIDTH:].T
            ya, va, q_t, k_stack, v_stack, vtb, kb, km, st = _even_in(
                x, row(mix_norm[i]), w_row, w_t, row(a_v_gain[j]), ws_tiled, bs_tiled, j, kv_stacks)
            kv_stacks = (k_stack, v_stack)
            kmean = km.reshape(N_TOK // MOBA_BLOCK, B_WIDTH)[:BATCH * N_BLOCKS]
            yb_t = _moba_prompt(q_t, kb, vtb, kmean.reshape(BATCH, N_BLOCKS, B_WIDTH))
            per_seq = jnp.transpose(st, (0, 3, 1, 2))
            padn = lambda a: jnp.pad(a, ((0, 0), (0, NEW_PAD - DEC_SEQ), (0, 0)))
            q_rep = jnp.repeat(per_seq[0], B_HEADS, axis=1)
            yb_s = _moba_sample(pt, q_rep, padn(per_seq[1]), padn(per_seq[2]), ck, cv, j)
            mix = (ya, yb_t, _step_major(yb_s).astype(bf16), even_w_out[j].astype(bf16))
            k_s.append(st[1].reshape(DEC_SEQ, B_HEADS, B_HEAD_DIM, DEC_BATCH))
            v_s.append(st[2].reshape(DEC_SEQ, B_HEADS, B_HEAD_DIM, DEC_BATCH))
            av_s.append(_batch_major(va))
        else:
            state = jnp.swapaxes(state_conv[j], 0, 1).reshape(CONV_FRONT, C_WIDTH)
            x, cp, cs = _odd(x, row(mix_norm[i]), odd_w_in[j].astype(bf16), odd_conv_w[j],
                             odd_w_out[j].astype(bf16), state)
            conv_p.append(cp[:, SUBLANES - (CONV_W - 1):])
            conv_s.append(_batch_major(cs, CONV_W - 1))
            mix = ()
        xs = _ffn((x,), row(ffn2_norm[i]), ffn2_gate, ffn2_up, ffn2_down, i, fg,
                  final=(i == DEPTH - 1), mix=mix)

    y_prompt = xs[0].reshape(BATCH, SEQ, D_MODEL)
    y_sample = _batch_major(xs[1])
    to_rows_p = lambda a: jnp.transpose(
        a.reshape(N_EVEN, BATCH, B_HEADS, B_HEAD_DIM, SEQ), (0, 1, 4, 2, 3))
    to_rows_s = lambda ts: jnp.transpose(jnp.stack(ts), (0, 4, 1, 2, 3))
    return (y_prompt, y_sample, to_rows_p(kv_stacks[0]), to_rows_p(kv_stacks[1]),
            jnp.stack(conv_p),
            to_rows_s(k_s), to_rows_s(v_s), jnp.stack(conv_s), jnp.stack(av_s))
```
